```python
import math
import jax
import jax.numpy as jnp
from jax import lax
import numpy as np

D_MODEL = 1024
BATCH = 16
SEQ = 2048
DEPTH = 2

N_META = 16
CHUNK = 128
EPS = 1e-6

SSD_HEADS = 16
SSD_HEAD_DIM = 64
SSD_INNER = SSD_HEADS * SSD_HEAD_DIM
SSD_GROUPS = 4
SSD_STATE = 128
SSD_CONV = 4
SSD_XBC = SSD_INNER + 2 * SSD_GROUPS * SSD_STATE

RET_HEADS = 4
RET_QK_DIM = 256
RET_V_DIM = 256
RET_QK_WIDTH = RET_HEADS * RET_QK_DIM
RET_WIDTH = RET_HEADS * RET_V_DIM

SB_HEADS = 16
SB_HEAD_DIM = 64
SB_WIDTH = SB_HEADS * SB_HEAD_DIM

LRU_WIDTH = 1024
LRU_BLOCKS = 8
LRU_BLOCK = LRU_WIDTH // LRU_BLOCKS
LRU_CONV = 4
LRU_C = 8.0

FFN_DIM = 2816
FFN_CONV = 3

MIX0_IN = SSD_INNER + SSD_XBC + SSD_HEADS + 2 * RET_QK_WIDTH + 2 * RET_WIDTH
MIX0_OUT = SSD_INNER + RET_WIDTH
MIX1_IN = 3 * SB_WIDTH + 2 * LRU_WIDTH
MIX1_OUT = SB_WIDTH + LRU_WIDTH

kernel_name = 'hybrid_ssd_retention_stickbreak_rglru_block'


def rmsnorm(x, g):
    xf = x.astype(jnp.float32)
    y = xf * lax.rsqrt(jnp.mean(xf * xf, -1, keepdims=True) + EPS)
    return (y * g).astype(x.dtype)


def causal_dwconv(x, w, b):
    K, C = w.shape
    y = lax.conv_general_dilated(x, w[:, None, :], window_strides=(1,), padding=[(K - 1, 0)],
                                 dimension_numbers=('NWC', 'WIO', 'NWC'), feature_group_count=C)
    return y + b


def split_cols(u, sizes):
    return jnp.split(u, np.cumsum(sizes)[:-1].tolist(), axis=-1)


def front_pad(t, pad):
    return jnp.pad(t, ((0, 0), (pad, 0)) + ((0, 0),) * (t.ndim - 2))


def segsum(a):
    L = a.shape[-1]
    cs = jnp.cumsum(a, -1)
    seg = cs[..., :, None] - cs[..., None, :]
    return jnp.where(jnp.tril(jnp.ones((L, L), bool)), seg, -jnp.inf)


def ssd_group(z, xbc, dt_raw, conv_w, conv_b, dt_bias, a_log, d_skip, norm_g):
    Bsz, T, _ = xbc.shape
    f32 = jnp.float32
    E = SSD_HEADS // SSD_GROUPS
    xbc = jax.nn.silu(causal_dwconv(xbc, conv_w, conv_b)).astype(f32)
    xs, bm, cm = jnp.split(xbc, [SSD_INNER, SSD_INNER + SSD_GROUPS * SSD_STATE], -1)
    dt = jax.nn.softplus(dt_raw.astype(f32) + dt_bias.astype(f32))
    a = -jnp.exp(a_log.astype(f32))
    pad = (-T) % CHUNK
    P = T + pad
    nc = P // CHUNK
    X = front_pad(xs * jnp.repeat(dt, SSD_HEAD_DIM, -1), pad).reshape(Bsz, nc, CHUNK, SSD_GROUPS, E, SSD_HEAD_DIM)
    Bm = front_pad(bm, pad).reshape(Bsz, nc, CHUNK, SSD_GROUPS, SSD_STATE)
    Cm = front_pad(cm, pad).reshape(Bsz, nc, CHUNK, SSD_GROUPS, SSD_STATE)
    a_dt = front_pad(dt * a, pad).reshape(Bsz, nc, CHUNK, SSD_GROUPS, E).transpose(0, 3, 4, 1, 2)
    a_cs = jnp.cumsum(a_dt, -1)
    cb = jnp.einsum('bclgn,bcsgn->bgcls', Cm, Bm)
    y_diag = jnp.einsum('bgecls,bcsgep->bclgep', cb[:, :, None] * jnp.exp(segsum(a_dt)), X)
    decay_states = jnp.exp(a_cs[..., -1:] - a_cs).transpose(0, 3, 4, 1, 2)
    states = jnp.einsum('bclgn,bclgep->bcgepn', Bm, X * decay_states[..., None])
    chunk_tot = jnp.pad(a_cs[..., -1], ((0, 0), (0, 0), (0, 0), (1, 0)))
    decay_chunk = jnp.exp(segsum(chunk_tot))
    states = jnp.concatenate([jnp.zeros_like(states[:, :1]), states], 1)
    prev = jnp.einsum('bgezc,bcgepn->bzgepn', decay_chunk, states)[:, :-1]
    y_off = jnp.einsum('bclgn,bcgepn->bclgep', Cm, prev) * jnp.exp(a_cs).transpose(0, 3, 4, 1, 2)[..., None]
    y = (y_diag + y_off).reshape(Bsz, P, SSD_INNER)[:, pad:]
    y = y + xs * jnp.repeat(d_skip.astype(f32), SSD_HEAD_DIM)
    y = (y * jax.nn.silu(z.astype(f32))).reshape(Bsz, T, SSD_GROUPS, SSD_INNER // SSD_GROUPS)
    y = y * lax.rsqrt(jnp.mean(y * y, -1, keepdims=True) + EPS)
    return (y.reshape(Bsz, T, SSD_INNER) * norm_g).astype(z.dtype)


def rotate_retnet(x, pos):
    half = x.shape[-1] // 2
    inv_freq = 1.0 / (10000.0 ** (jnp.arange(half, dtype=jnp.float32) / (half - 1)))
    ang = pos[:, None] * inv_freq[None, :]
    cos = jnp.cos(ang)[None, :, None, :]
    sin = jnp.sin(ang)[None, :, None, :]
    x1, x2 = x[..., :half], x[..., half:]
    return jnp.concatenate([x1 * cos - x2 * sin, x1 * sin + x2 * cos], -1)


def retention_group(q, k, v, g, norm_g):
    Bsz, T, _ = q.shape
    f32 = jnp.float32
    pos = jnp.arange(T, dtype=f32)
    q = rotate_retnet(q.astype(f32).reshape(Bsz, T, RET_HEADS, RET_QK_DIM), pos)
    k = rotate_retnet(k.astype(f32).reshape(Bsz, T, RET_HEADS, RET_QK_DIM), pos) * RET_QK_DIM ** -0.5
    v = v.astype(f32).reshape(Bsz, T, RET_HEADS, RET_V_DIM)
    pad = (-T) % CHUNK
    P = T + pad
    nc = P // CHUNK
    q = front_pad(q, pad).reshape(Bsz, nc, CHUNK, RET_HEADS, RET_QK_DIM)
    k = front_pad(k, pad).reshape(Bsz, nc, CHUNK, RET_HEADS, RET_QK_DIM)
    v = front_pad(v, pad).reshape(Bsz, nc, CHUNK, RET_HEADS, RET_V_DIM)
    log_gamma = jnp.log1p(-jnp.exp2(-5.0 - jnp.arange(RET_HEADS, dtype=f32)))
    idx = jnp.arange(CHUNK, dtype=f32)
    diff = idx[:, None] - idx[None, :]
    decay = jnp.where(diff >= 0, jnp.exp(log_gamma[:, None, None] * jnp.maximum(diff, 0.0)), 0.0)
    scores = jnp.einsum('bclhd,bcshd->bhcls', q, k) * decay[None, :, None]
    inner = jnp.einsum('bhcls,bcshe->bclhe', scores, v)
    zeta = jnp.exp(log_gamma[None, :] * (CHUNK - 1 - idx)[:, None])
    kv = jnp.einsum('bclhd,bclhe->bchde', k * zeta[..., None], v)
    chunk_decay = jnp.exp(CHUNK * log_gamma)[None, :, None, None]

    def step(R, kv_c):
        return chunk_decay * R + kv_c, R

    _, R_prev = lax.scan(step, jnp.zeros_like(kv[:, 0]), jnp.moveaxis(kv, 1, 0))
    R_prev = jnp.moveaxis(R_prev, 0, 1)
    xi = jnp.exp(log_gamma[None, :] * (idx + 1.0)[:, None])
    cross = jnp.einsum('bclhd,bchde->bclhe', q, R_prev) * xi[..., None]
    o = (inner + cross).reshape(Bsz, P, RET_HEADS, RET_V_DIM)[:, pad:]
    o = o - jnp.mean(o, -1, keepdims=True)
    o = o * lax.rsqrt(jnp.mean(o * o, -1, keepdims=True) + EPS)
    o = o.reshape(Bsz, T, RET_WIDTH) * norm_g
    return (jax.nn.silu(g.astype(f32)) * o).astype(g.dtype)


def stick_breaking_attention(q, k, v):
    Bsz, T, H, D = q.shape
    f32 = jnp.float32
    scale = D ** -0.5
    bounds = [(0, N_META)] + [(s, min(s + CHUNK, T)) for s in range(N_META, T, CHUNK)]
    outs = []
    for s0, s1 in bounds:
        z = jnp.einsum('bqhd,bkhd->bhqk', q[:, s0:s1].astype(f32), k[:, :s1].astype(f32)) * scale
        strict = jnp.arange(s1)[None, :] < jnp.arange(s0, s1)[:, None]
        log_1m = jnp.where(strict, jax.nn.log_sigmoid(-z), 0.0)
        after = lax.cumsum(log_1m, axis=3, reverse=True) - log_1m
        w = jnp.where(strict, jnp.exp(jax.nn.log_sigmoid(z) + after), 0.0)
        outs.append(jnp.einsum('bhqk,bkhd->bqhd', w, v[:, :s1].astype(f32)))
    return jnp.concatenate(outs, 1)


def rg_lru(x, w_a, b_a, w_x, b_x, lam):
    Bsz, T, W = x.shape
    f32 = jnp.float32
    x = x.astype(f32)
    xb = x.reshape(Bsz, T, LRU_BLOCKS, LRU_BLOCK)
    r = jax.nn.sigmoid(jnp.einsum('btni,nij->btnj', xb, w_a.astype(f32)).reshape(Bsz, T, W) + b_a)
    i = jax.nn.sigmoid(jnp.einsum('btni,nij->btnj', xb, w_x.astype(f32)).reshape(Bsz, T, W) + b_x)
    log_a = -LRU_C * r * jax.nn.softplus(-lam.astype(f32))
    a = jnp.exp(log_a)
    b = jnp.sqrt(jnp.maximum(-jnp.expm1(2.0 * log_a), 0.0)) * (i * x)

    def combine(left, right):
        a_l, b_l = left
        a_r, b_r = right
        return a_l * a_r, a_r * b_l + b_r

    _, hs = lax.associative_scan(combine, (a, b), axis=1)
    return hs


def ssd_retention_mixer(h, w_in, ssd_conv_w, ssd_conv_b, ssd_dt_bias, ssd_a_log, ssd_d, ssd_norm, ret_norm, w_out):
    u = h @ w_in
    z, xbc, dt_raw, q, k, v, g = split_cols(u, (SSD_INNER, SSD_XBC, SSD_HEADS, RET_QK_WIDTH, RET_QK_WIDTH, RET_WIDTH, RET_WIDTH))
    y_ssd = ssd_group(z, xbc, dt_raw, ssd_conv_w, ssd_conv_b, ssd_dt_bias, ssd_a_log, ssd_d, ssd_norm)
    y_ret = retention_group(q, k, v, g, ret_norm)
    return jnp.concatenate([y_ssd, y_ret], -1) @ w_out


def sb_lru_mixer(h, w_in, lru_conv_w, lru_conv_b, lru_wa, lru_ba, lru_wx, lru_bx, lru_lambda, w_out):
    Bsz, T, _ = h.shape
    u = h @ w_in
    q, k, v, gate, xr = split_cols(u, (SB_WIDTH, SB_WIDTH, SB_WIDTH, LRU_WIDTH, LRU_WIDTH))
    shp = (Bsz, T, SB_HEADS, SB_HEAD_DIM)
    y_sb = stick_breaking_attention(q.reshape(shp), k.reshape(shp), v.reshape(shp)).reshape(Bsz, T, SB_WIDTH)
    xr = causal_dwconv(xr, lru_conv_w, lru_conv_b)
    y_lru = rg_lru(xr, lru_wa, lru_ba, lru_wx, lru_bx, lru_lambda) * jax.nn.gelu(gate.astype(jnp.float32))
    return jnp.concatenate([y_sb.astype(h.dtype), y_lru.astype(h.dtype)], -1) @ w_out


def conv_ffn(h, w_in, conv_w, conv_b, w_out):
    u = causal_dwconv(h @ w_in, conv_w, conv_b)
    g, up = jnp.split(u, 2, -1)
    return (jax.nn.silu(g) * up) @ w_out


def setup_inputs(seed: int = 0) -> dict:
    key = jax.random.key(seed)
    ks = iter(jax.random.split(key, 64))
    f32 = jnp.float32

    def nrm(shape, scale):
        return jax.random.normal(next(ks), shape, f32) * scale

    def gain(n):
        return 1.0 + nrm((n,), 0.02)

    dt0 = jnp.exp(jax.random.uniform(next(ks), (SSD_HEADS,), f32, math.log(1e-3), math.log(1e-1)))
    dt_bias = dt0 + jnp.log(-jnp.expm1(-dt0))
    a_log = jnp.log(jax.random.uniform(next(ks), (SSD_HEADS,), f32, 1.0, 16.0))
    a_lru = jax.random.uniform(next(ks), (LRU_WIDTH,), f32, 0.9, 0.999) ** (1.0 / LRU_C)
    lam = jnp.log(a_lru) - jnp.log1p(-a_lru)
    return {
        'x': nrm((BATCH, SEQ, D_MODEL), 1.0),
        'meta_tokens': nrm((N_META, D_MODEL), 1.0),
        'l0_mix_norm': gain(D_MODEL),
        'l0_w_in': nrm((D_MODEL, MIX0_IN), D_MODEL ** -0.5),
        'l0_ssd_conv_w': nrm((SSD_CONV, SSD_XBC), SSD_CONV ** -0.5),
        'l0_ssd_conv_b': nrm((SSD_XBC,), 0.01),
        'l0_ssd_dt_bias': dt_bias,
        'l0_ssd_a_log': a_log,
        'l0_ssd_d': gain(SSD_HEADS),
        'l0_ssd_norm': gain(SSD_INNER),
        'l0_ret_norm': gain(RET_WIDTH),
        'l0_w_out': nrm((MIX0_OUT, D_MODEL), MIX0_OUT ** -0.5),
        'l0_ffn_norm': gain(D_MODEL),
        'l0_ffn_w_in': nrm((D_MODEL, 2 * FFN_DIM), D_MODEL ** -0.5),
        'l0_ffn_conv_w': nrm((FFN_CONV, 2 * FFN_DIM), FFN_CONV ** -0.5),
        'l0_ffn_conv_b': nrm((2 * FFN_DIM,), 0.01),
        'l0_ffn_w_out': nrm((FFN_DIM, D_MODEL), FFN_DIM ** -0.5),
        'l1_mix_norm': gain(D_MODEL),
        'l1_w_in': nrm((D_MODEL, MIX1_IN), D_MODEL ** -0.5),
        'l1_lru_conv_w': nrm((LRU_CONV, LRU_WIDTH), LRU_CONV ** -0.5),
        'l1_lru_conv_b': nrm((LRU_WIDTH,), 0.01),
        'l1_lru_wa': nrm((LRU_BLOCKS, LRU_BLOCK, LRU_BLOCK), LRU_BLOCK ** -0.5),
        'l1_lru_ba': nrm((LRU_WIDTH,), 0.01),
        'l1_lru_wx': nrm((LRU_BLOCKS, LRU_BLOCK, LRU_BLOCK), LRU_BLOCK ** -0.5),
        'l1_lru_bx': nrm((LRU_WIDTH,), 0.01),
        'l1_lru_lambda': lam,
        'l1_w_out': nrm((MIX1_OUT, D_MODEL), MIX1_OUT ** -0.5),
        'l1_ffn_norm': gain(D_MODEL),
        'l1_ffn_w_in': nrm((D_MODEL, 2 * FFN_DIM), D_MODEL ** -0.5),
        'l1_ffn_conv_w': nrm((FFN_CONV, 2 * FFN_DIM), FFN_CONV ** -0.5),
        'l1_ffn_conv_b': nrm((2 * FFN_DIM,), 0.01),
        'l1_ffn_w_out': nrm((FFN_DIM, D_MODEL), FFN_DIM ** -0.5),
        'final_norm': gain(D_MODEL),
    }


def reference(x, meta_tokens, l0_mix_norm, l0_w_in, l0_ssd_conv_w, l0_ssd_conv_b, l0_ssd_dt_bias, l0_ssd_a_log,
              l0_ssd_d, l0_ssd_norm, l0_ret_norm, l0_w_out, l0_ffn_norm, l0_ffn_w_in, l0_ffn_conv_w, l0_ffn_conv_b,
              l0_ffn_w_out, l1_mix_norm, l1_w_in, l1_lru_conv_w, l1_lru_conv_b, l1_lru_wa, l1_lru_ba, l1_lru_wx,
              l1_lru_bx, l1_lru_lambda, l1_w_out, l1_ffn_norm, l1_ffn_w_in, l1_ffn_conv_w, l1_ffn_conv_b,
              l1_ffn_w_out, final_norm):
    Bsz = x.shape[0]
    meta = jnp.broadcast_to(meta_tokens[None].astype(x.dtype), (Bsz, N_META, D_MODEL))
    h = jnp.concatenate([meta, x], 1)
    even_layers = [(l0_mix_norm,
                    (l0_w_in, l0_ssd_conv_w, l0_ssd_conv_b, l0_ssd_dt_bias, l0_ssd_a_log, l0_ssd_d, l0_ssd_norm, l0_ret_norm, l0_w_out),
                    l0_ffn_norm, (l0_ffn_w_in, l0_ffn_conv_w, l0_ffn_conv_b, l0_ffn_w_out))]
    odd_layers = [(l1_mix_norm,
                   (l1_w_in, l1_lru_conv_w, l1_lru_conv_b, l1_lru_wa, l1_lru_ba, l1_lru_wx, l1_lru_bx, l1_lru_lambda, l1_w_out),
                   l1_ffn_norm, (l1_ffn_w_in, l1_ffn_conv_w, l1_ffn_conv_b, l1_ffn_w_out))]
    for layer in range(DEPTH):
        if layer % 2 == 0:
            mix_norm, mix_params, ffn_norm, ffn_params = even_layers[layer // 2]
            h = h + ssd_retention_mixer(rmsnorm(h, mix_norm), *mix_params)
        else:
            mix_norm, mix_params, ffn_norm, ffn_params = odd_layers[layer // 2]
            h = h + sb_lru_mixer(rmsnorm(h, mix_norm), *mix_params)
        h = h + conv_ffn(rmsnorm(h, ffn_norm), *ffn_params)
    return rmsnorm(h, final_norm)[:, N_META:]
```

```python
import functools
import math

import jax
import jax.numpy as jnp
from jax import lax
from jax.experimental import pallas as pl
from jax.experimental.pallas import tpu as pltpu

F32 = jnp.float32
BF16 = jnp.bfloat16

D_MODEL = 1024
N_META = 16
CHUNK = 128
PAD_ROWS = CHUNK - N_META
EPS = 1e-6

SSD_HEADS = 16
SSD_HEAD_DIM = 64
SSD_INNER = SSD_HEADS * SSD_HEAD_DIM
SSD_GROUPS = 4
SSD_STATE = 128
SSD_CONV = 4
GROUP_W = SSD_INNER // SSD_GROUPS
HEADS_PER_GROUP = SSD_HEADS // SSD_GROUPS

RET_HEADS = 4
RET_DIM = 256

SB_HEADS = 16
SB_HEAD_DIM = 64
SB_WIDTH = SB_HEADS * SB_HEAD_DIM
SB_TILE = 256

LRU_WIDTH = 1024
LRU_BLOCKS = 8
LRU_BLOCK = LRU_WIDTH // LRU_BLOCKS
LRU_CONV = 4
LRU_C = 8.0

FFN_DIM = 2816
FFN_CONV = 3
FFN_SPLIT = 2
FFN_CHUNK = FFN_DIM // FFN_SPLIT

LANE = 128
HALO = 8
MIB = 1024 * 1024


def _params(sem, vmem_mib):
    return pltpu.CompilerParams(dimension_semantics=sem, vmem_limit_bytes=vmem_mib * MIB)


def _row_tile(n, cap, mult=16):
    best = None
    for t in range(mult, min(n, cap) + 1, mult):
        if n % t == 0:
            best = t
    assert best is not None, (n, cap)
    return best


def _sigmoid(x):
    return 1.0 / (1.0 + jnp.exp(-x))


def _silu(x):
    return x * _sigmoid(x)


def _softplus(x):
    return jnp.maximum(x, 0.0) + jnp.log1p(jnp.exp(-jnp.abs(x)))


def _dot(a, b):
    return jnp.dot(a, b, preferred_element_type=F32)


def _dot_nt(a, b):
    return lax.dot_general(a, b, (((1,), (1,)), ((), ())), preferred_element_type=F32)


def _split3_dot(tri_bf16, x):
    x1 = x.astype(BF16)
    r1 = x - x1.astype(F32)
    x2 = r1.astype(BF16)
    x3 = (r1 - x2.astype(F32)).astype(BF16)
    return _dot(tri_bf16, x1) + _dot(tri_bf16, x2) + _dot(tri_bf16, x3)


def _norm_matmul_kernel(x_ref, g_ref, w_ref, *rest, has_aux):
    if has_aux:
        waux_ref, o_ref, aux_ref, xn_ref = rest
    else:
        o_ref, xn_ref = rest

    @pl.when(pl.program_id(1) == 0)
    def _():
        x = x_ref[...]
        ms = jnp.mean(x * x, axis=-1, keepdims=True)
        xn_ref[...] = ((x * lax.rsqrt(ms + EPS)) * g_ref[...]).astype(BF16)
        if has_aux:
            aux_ref[...] = _dot(xn_ref[...], waux_ref[...])

    o_ref[...] = _dot(xn_ref[...], w_ref[...]).astype(o_ref.dtype)


def _norm_matmul(h2d, gain, w, w_aux=None):
    m, d = h2d.shape
    n = w.shape[1]
    tm = _row_tile(m, 1024)
    tn = 1024
    assert n % tn == 0
    has_aux = w_aux is not None
    in_specs = [
        pl.BlockSpec((tm, d), lambda i, j: (i, 0)),
        pl.BlockSpec((1, d), lambda i, j: (0, 0)),
        pl.BlockSpec((d, tn), lambda i, j: (0, j)),
    ]
    out_shape = [jax.ShapeDtypeStruct((m, n), BF16)]
    out_specs = [pl.BlockSpec((tm, tn), lambda i, j: (i, j))]
    args = [h2d, gain.reshape(1, d), w]
    if has_aux:
        in_specs.append(pl.BlockSpec((d, LANE), lambda i, j: (0, 0)))
        out_shape.append(jax.ShapeDtypeStruct((m, LANE), F32))
        out_specs.append(pl.BlockSpec((tm, LANE), lambda i, j: (i, 0)))
        args.append(w_aux)
    outs = pl.pallas_call(
        functools.partial(_norm_matmul_kernel, has_aux=has_aux),
        grid=(m // tm, n // tn),
        in_specs=in_specs,
        out_specs=out_specs,
        out_shape=out_shape,
        scratch_shapes=[pltpu.VMEM((tm, d), BF16)],
        compiler_params=_params(("parallel", "arbitrary"), 40),
        name="norm_matmul_aux" if has_aux else "norm_matmul",
    )(*args)
    return outs if has_aux else outs[0]


def _valid_rows(chunk_idx, rows):
    r = lax.broadcasted_iota(jnp.int32, (rows, 1), 0)
    return jnp.logical_or(chunk_idx > 0, r >= PAD_ROWS)


def _causal_conv(buf_ref, raw, w_ref, b_ref, taps, rows):
    buf_ref[HALO:HALO + rows, :] = raw
    out = b_ref[...]
    for k in range(taps):
        off = HALO - (taps - 1) + k
        out = out + w_ref[k:k + 1, :] * buf_ref[off:off + rows, :]
    buf_ref[0:HALO, :] = raw[rows - HALO:rows, :]
    return out


def _rep_heads(cols, g):
    rows = cols.shape[0]
    lane = lax.broadcasted_iota(jnp.int32, (rows, GROUP_W), 1)
    h0 = g * HEADS_PER_GROUP
    out = jnp.broadcast_to(cols[:, h0 + 3:h0 + 4], (rows, GROUP_W))
    for hh in (2, 1, 0):
        out = jnp.where(lane < (hh + 1) * SSD_HEAD_DIM, cols[:, h0 + hh:h0 + hh + 1], out)
    return out


def _ssd_kernel(z_ref, xs_ref, bc_ref, dt_ref, cwx_ref, cwbc_ref, cbx_ref, cbbc_ref, dtb_ref, alog_ref,
                dskip_ref, ng_ref, o_ref, bufx, bufbc, state):
    c = pl.program_id(1)
    L = CHUNK

    @pl.when(c == 0)
    def _():
        bufx[0:HALO, :] = jnp.zeros((HALO, SSD_INNER), F32)
        bufbc[0:HALO, :] = jnp.zeros((HALO, 2 * SSD_GROUPS * SSD_STATE), F32)
        state[...] = jnp.zeros(state.shape, F32)

    valid = _valid_rows(c, L)
    xs = _causal_conv(bufx, xs_ref[0].astype(F32), cwx_ref, cbx_ref, SSD_CONV, L)
    xs = jnp.where(valid, _silu(xs), 0.0)
    bc = _causal_conv(bufbc, bc_ref[0].astype(F32), cwbc_ref, cbbc_ref, SSD_CONV, L)
    bc = jnp.where(valid, _silu(bc), 0.0)

    dt = jnp.where(valid, _softplus(dt_ref[0] + dtb_ref[...]), 0.0)
    a_dt = dt * (-jnp.exp(alog_ref[...]))
    row = lax.broadcasted_iota(jnp.int32, (L, L), 0)
    col = lax.broadcasted_iota(jnp.int32, (L, L), 1)
    lower = row >= col
    a_col = _split3_dot(lower.astype(BF16), a_dt)
    a_row = a_col.T
    lane = lax.broadcasted_iota(jnp.int32, (L, GROUP_W), 1)

    for g in range(SSD_GROUPS):
        sl = slice(g * GROUP_W, (g + 1) * GROUP_W)
        bm = bc[:, g * SSD_STATE:(g + 1) * SSD_STATE]
        cm = bc[:, (SSD_GROUPS + g) * SSD_STATE:(SSD_GROUPS + g + 1) * SSD_STATE]
        bm16 = bm.astype(BF16)
        cm16 = cm.astype(BF16)
        cb = _dot_nt(cm16, bm16)
        a_rep = _rep_heads(a_col, g)
        xg = xs[:, sl]
        xdt = xg * _rep_heads(dt, g)
        xdt16 = xdt.astype(BF16)
        y = jnp.zeros((L, GROUP_W), F32)
        for hh in range(HEADS_PER_GROUP):
            h = g * HEADS_PER_GROUP + hh
            seg = a_col[:, h:h + 1] - a_row[h:h + 1, :]
            decay = jnp.where(lower, jnp.exp(seg), 0.0)
            yh = _dot((cb * decay).astype(BF16), xdt16)
            y = jnp.where(jnp.logical_and(lane >= hh * SSD_HEAD_DIM, lane < (hh + 1) * SSD_HEAD_DIM), yh, y)
        s_prev = state[g]
        y = y + _dot(cm16, s_prev.astype(BF16)) * jnp.exp(a_rep)
        a_last = a_rep[L - 1:L, :]
        xdec = (xdt * jnp.exp(a_last - a_rep)).astype(BF16)
        state[g] = s_prev * jnp.exp(a_last) + _dot(bm.T.astype(BF16), xdec)
        y = y + xg * dskip_ref[:, sl]
        y = y * _silu(z_ref[0, :, sl].astype(F32))
        y = y * lax.rsqrt(jnp.mean(y * y, axis=-1, keepdims=True) + EPS)
        o_ref[0, :, sl] = (y * ng_ref[:, sl]).astype(o_ref.dtype)


def _ssd(u, dt_raw, conv_w, conv_b, dt_bias, a_log, d_skip, norm_g):
    bsz, p, _ = u.shape
    nc = p // CHUNK
    w = SSD_INNER

    def ublock(idx):
        return pl.BlockSpec((1, CHUNK, w), lambda b, c: (b, c, idx))

    def full(shape):
        return pl.BlockSpec(shape, lambda b, c: (0,) * len(shape))

    pad_heads = LANE - SSD_HEADS
    args = (
        u, u, u, dt_raw,
        conv_w[:, :w], conv_w[:, w:], conv_b[:w].reshape(1, w), conv_b[w:].reshape(1, w),
        jnp.pad(dt_bias, (0, pad_heads)).reshape(1, LANE), jnp.pad(a_log, (0, pad_heads)).reshape(1, LANE),
        jnp.repeat(d_skip, SSD_HEAD_DIM).reshape(1, w), norm_g.reshape(1, w),
    )
    in_specs = [
        ublock(0), ublock(1), ublock(2),
        pl.BlockSpec((1, CHUNK, LANE), lambda b, c: (b, c, 0)),
        full((SSD_CONV, w)), full((SSD_CONV, w)), full((1, w)), full((1, w)),
        full((1, LANE)), full((1, LANE)), full((1, w)), full((1, w)),
    ]
    return pl.pallas_call(
        _ssd_kernel,
        grid=(bsz, nc),
        in_specs=in_specs,
        out_specs=pl.BlockSpec((1, CHUNK, w), lambda b, c: (b, c, 0)),
        out_shape=jax.ShapeDtypeStruct((bsz, p, w), BF16),
        scratch_shapes=[
            pltpu.VMEM((HALO + CHUNK, w), F32),
            pltpu.VMEM((HALO + CHUNK, w), F32),
            pltpu.VMEM((SSD_GROUPS, SSD_STATE, GROUP_W), F32),
        ],
        compiler_params=_params(("parallel", "arbitrary"), 32),
        name="ssd_scan",
    )(*args)


def _rotate(x, cos, sin):
    half = RET_DIM // 2
    x1, x2 = x[:, :half], x[:, half:]
    return jnp.concatenate([x1 * cos - x2 * sin, x1 * sin + x2 * cos], axis=-1)


def _ret_kernel(q_ref, k_ref, v_ref, g_ref, cos_ref, sin_ref, dec_ref, zeta_ref, xi_ref, cd_ref, ng_ref,
                o_ref, state):
    c = pl.program_id(1)
    L = CHUNK

    @pl.when(c == 0)
    def _():
        state[...] = jnp.zeros(state.shape, F32)

    valid = _valid_rows(c, L)
    cos = cos_ref[...]
    sin = sin_ref[...]
    for h in range(RET_HEADS):
        sl = slice(h * RET_DIM, (h + 1) * RET_DIM)
        q = jnp.where(valid, _rotate(q_ref[0, :, sl].astype(F32), cos, sin), 0.0)
        k = jnp.where(valid, _rotate(k_ref[0, :, sl].astype(F32), cos, sin) * (RET_DIM ** -0.5), 0.0)
        v16 = jnp.where(valid, v_ref[0, :, sl], jnp.zeros((), BF16))
        q16 = q.astype(BF16)
        scores = _dot_nt(q16, k.astype(BF16)) * dec_ref[h]
        r_prev = state[h]
        o = _dot(scores.astype(BF16), v16) + _dot(q16, r_prev.astype(BF16)) * xi_ref[:, h:h + 1]
        kz = (k * zeta_ref[:, h:h + 1]).T.astype(BF16)
        state[h] = r_prev * cd_ref[:, h:h + 1] + _dot(kz, v16)
        o = o - jnp.mean(o, axis=-1, keepdims=True)
        o = o * lax.rsqrt(jnp.mean(o * o, axis=-1, keepdims=True) + EPS)
        o = o * ng_ref[:, sl]
        o_ref[0, :, sl] = (_silu(g_ref[0, :, sl].astype(F32)) * o).astype(o_ref.dtype)


def _retention(u, norm_g):
    bsz, p, _ = u.shape
    nc = p // CHUNK
    w = RET_HEADS * RET_DIM
    half = RET_DIM // 2
    pos = jnp.arange(p, dtype=F32) - PAD_ROWS
    inv_freq = 1.0 / (10000.0 ** (jnp.arange(half, dtype=F32) / (half - 1)))
    ang = pos[:, None] * inv_freq[None, :]
    log_gamma = jnp.log1p(-jnp.exp2(-5.0 - jnp.arange(RET_HEADS, dtype=F32)))
    idx = jnp.arange(CHUNK, dtype=F32)
    diff = idx[:, None] - idx[None, :]
    decay = jnp.where(diff >= 0, jnp.exp(log_gamma[:, None, None] * jnp.maximum(diff, 0.0)), 0.0)
    pad_heads = LANE - RET_HEADS
    zeta = jnp.pad(jnp.exp(log_gamma[None, :] * (CHUNK - 1 - idx)[:, None]), ((0, 0), (0, pad_heads)))
    xi = jnp.pad(jnp.exp(log_gamma[None, :] * (idx + 1.0)[:, None]), ((0, 0), (0, pad_heads)))
    chunk_decay = jnp.pad(jnp.exp(CHUNK * log_gamma), (0, pad_heads)).reshape(1, LANE)

    def ublock(idx_):
        return pl.BlockSpec((1, CHUNK, w), lambda b, c: (b, c, idx_))

    def full(shape):
        return pl.BlockSpec(shape, lambda b, c: (0,) * len(shape))

    return pl.pallas_call(
        _ret_kernel,
        grid=(bsz, nc),
        in_specs=[
            ublock(3), ublock(4), ublock(5), ublock(6),
            pl.BlockSpec((CHUNK, half), lambda b, c: (c, 0)),
            pl.BlockSpec((CHUNK, half), lambda b, c: (c, 0)),
            full((RET_HEADS, CHUNK, CHUNK)), full((CHUNK, LANE)), full((CHUNK, LANE)), full((1, LANE)),
            full((1, w)),
        ],
        out_specs=pl.BlockSpec((1, CHUNK, w), lambda b, c: (b, c, 0)),
        out_shape=jax.ShapeDtypeStruct((bsz, p, w), BF16),
        scratch_shapes=[pltpu.VMEM((RET_HEADS, RET_DIM, RET_DIM), F32)],
        compiler_params=_params(("parallel", "arbitrary"), 32),
        name="retention",
    )(u, u, u, u, jnp.cos(ang), jnp.sin(ang), decay, zeta, xi, chunk_decay, norm_g.reshape(1, w))


def _outproj_kernel(y1_ref, y2_ref, w1_ref, w2_ref, h_ref, o_ref):
    tm = o_ref.shape[1]
    acc = h_ref[0] + _dot(y1_ref[0], w1_ref[...]) + _dot(y2_ref[0], w2_ref[...])
    r = pl.program_id(1) * tm + lax.broadcasted_iota(jnp.int32, (tm, 1), 0)
    o_ref[0] = jnp.where(r >= PAD_ROWS, acc, 0.0)


def _outproj(y1, y2, w1, w2, h):
    bsz, p, d = h.shape
    tm = _row_tile(p, 544)
    k1, k2 = y1.shape[-1], y2.shape[-1]
    return pl.pallas_call(
        _outproj_kernel,
        grid=(bsz, p // tm),
        in_specs=[
            pl.BlockSpec((1, tm, k1), lambda b, t: (b, t, 0)),
            pl.BlockSpec((1, tm, k2), lambda b, t: (b, t, 0)),
            pl.BlockSpec((k1, d), lambda b, t: (0, 0)),
            pl.BlockSpec((k2, d), lambda b, t: (0, 0)),
            pl.BlockSpec((1, tm, d), lambda b, t: (b, t, 0)),
        ],
        out_specs=pl.BlockSpec((1, tm, d), lambda b, t: (b, t, 0)),
        out_shape=jax.ShapeDtypeStruct((bsz, p, d), F32),
        compiler_params=_params(("parallel", "parallel"), 40),
        name="outproj_residual",
    )(y1, y2, w1, w2, h)


def _ffn_kernel(h_ref, gain_ref, wg_ref, wu_ref, cwg_ref, cwu_ref, cbg_ref, cbu_ref, wo_ref, o_ref,
                xn, acc, bufg, bufu):
    s = pl.program_id(1)
    t = pl.program_id(2)
    tm = o_ref.shape[1]
    rows = pl.ds(pl.multiple_of(t * tm, tm), tm)

    @pl.when(s == 0)
    def _():
        x = h_ref[0]
        ms = jnp.mean(x * x, axis=-1, keepdims=True)
        xn[rows, :] = ((x * lax.rsqrt(ms + EPS)) * gain_ref[...]).astype(BF16)
        acc[rows, :] = x

    @pl.when(t == 0)
    def _():
        bufg[0:HALO, :] = jnp.zeros((HALO, FFN_CHUNK), F32)
        bufu[0:HALO, :] = jnp.zeros((HALO, FFN_CHUNK), F32)

    xt = xn[rows, :]
    gate = _causal_conv(bufg, _dot(xt, wg_ref[...]), cwg_ref, cbg_ref, FFN_CONV, tm)
    up = _causal_conv(bufu, _dot(xt, wu_ref[...]), cwu_ref, cbu_ref, FFN_CONV, tm)
    act = (_silu(gate) * up).astype(BF16)
    acc[rows, :] += _dot(act, wo_ref[...])

    @pl.when(s == FFN_SPLIT - 1)
    def _():
        r = t * tm + lax.broadcasted_iota(jnp.int32, (tm, 1), 0)
        o_ref[0] = jnp.where(r >= PAD_ROWS, acc[rows, :], 0.0)


def _conv_ffn(h, gain, w_in, conv_w, conv_b, w_out):
    bsz, p, d = h.shape
    tm = _row_tile(p, 272)
    nt = p // tm
    fc = FFN_CHUNK
    cb = conv_b.reshape(1, 2 * FFN_DIM)
    return pl.pallas_call(
        _ffn_kernel,
        grid=(bsz, FFN_SPLIT, nt),
        in_specs=[
            pl.BlockSpec((1, tm, d), lambda b, s, t: (b, t * (1 - s), 0)),
            pl.BlockSpec((1, d), lambda b, s, t: (0, 0)),
            pl.BlockSpec((d, fc), lambda b, s, t: (0, s)),
            pl.BlockSpec((d, fc), lambda b, s, t: (0, FFN_SPLIT + s)),
            pl.BlockSpec((FFN_CONV, fc), lambda b, s, t: (0, s)),
            pl.BlockSpec((FFN_CONV, fc), lambda b, s, t: (0, FFN_SPLIT + s)),
            pl.BlockSpec((1, fc), lambda b, s, t: (0, s)),
            pl.BlockSpec((1, fc), lambda b, s, t: (0, FFN_SPLIT + s)),
            pl.BlockSpec((fc, d), lambda b, s, t: (s, 0)),
        ],
        out_specs=pl.BlockSpec((1, tm, d), lambda b, s, t: (b, t * (s // (FFN_SPLIT - 1)), 0)),
        out_shape=jax.ShapeDtypeStruct((bsz, p, d), F32),
        scratch_shapes=[
            pltpu.VMEM((p, d), BF16),
            pltpu.VMEM((p, d), F32),
            pltpu.VMEM((HALO + tm, fc), F32),
            pltpu.VMEM((HALO + tm, fc), F32),
        ],
        compiler_params=_params(("parallel", "arbitrary", "arbitrary"), 56),
        name="conv_ffn",
    )(h, gain.reshape(1, d), w_in, w_in, conv_w, conv_w, cb, cb, w_out)


def _sb_tile(z, mask, carry, acc, v16, upper16):
    log1m = -_softplus(z)
    if mask is not None:
        log1m = jnp.where(mask, log1m, 0.0)
    hi = log1m.astype(BF16)
    lo = (log1m - hi.astype(F32)).astype(BF16)
    after = _dot(hi, upper16) + _dot(lo, upper16)
    w = jnp.exp(z + log1m + after + carry)
    if mask is not None:
        w = jnp.where(mask, w, 0.0)
    acc = acc + _dot(w.astype(BF16), v16)
    carry = carry + after[:, 0:1] + log1m[:, 0:1]
    return carry, acc


def _sb_kernel(q_ref, k_ref, v_ref, o_ref, *, nq):
    tq = SB_TILE

    def iotas(rows, cols):
        return lax.broadcasted_iota(jnp.int32, (rows, cols), 0), lax.broadcasted_iota(jnp.int32, (rows, cols), 1)

    rr, cc = iotas(tq, tq)
    upper = (rr > cc).astype(BF16)
    strict = cc < rr
    rr_c, cc_c = iotas(CHUNK, CHUNK)
    upper_c = (rr_c > cc_c).astype(BF16)
    scale = SB_HEAD_DIM ** -0.5

    def first_head(rows):
        return lax.broadcasted_iota(jnp.int32, (rows, LANE), 1) < SB_HEAD_DIM

    def head_q(q16, hd):
        keep = first_head(q16.shape[0])
        if hd == 1:
            keep = jnp.logical_not(keep)
        return jnp.where(keep, q16.astype(F32) * scale, 0.0).astype(BF16)

    q0 = q_ref[0, 0:CHUNK, :]
    k0 = k_ref[0, 0:CHUNK, :]
    v0 = v_ref[0, 0:CHUNK, :]
    meta_mask = jnp.logical_and(cc_c < rr_c, cc_c >= PAD_ROWS)
    outs = []
    for hd in range(2):
        z = _dot_nt(head_q(q0, hd), k0)
        _, acc = _sb_tile(z, meta_mask, jnp.zeros((CHUNK, 1), F32), jnp.zeros((CHUNK, LANE), F32), v0, upper_c)
        outs.append(acc)
    o_ref[0, 0:CHUNK, :] = jnp.where(first_head(CHUNK), outs[0], outs[1]).astype(o_ref.dtype)

    meta_key_q = lax.broadcasted_iota(jnp.int32, (tq, CHUNK), 1) >= PAD_ROWS

    def q_body(i, _):
        r0 = pl.multiple_of(CHUNK + i * tq, CHUNK)
        q16 = q_ref[0, pl.ds(r0, tq), :]
        kd = k_ref[0, pl.ds(r0, tq), :]
        vd = v_ref[0, pl.ds(r0, tq), :]
        res = []
        for hd in range(2):
            qh = head_q(q16, hd)
            carry, acc = _sb_tile(_dot_nt(qh, kd), strict, jnp.zeros((tq, 1), F32),
                                  jnp.zeros((tq, LANE), F32), vd, upper)

            def k_body(n, st, qh=qh):
                rk = pl.multiple_of(CHUNK + (i - 1 - n) * tq, CHUNK)
                kt = k_ref[0, pl.ds(rk, tq), :]
                vt = v_ref[0, pl.ds(rk, tq), :]
                return _sb_tile(_dot_nt(qh, kt), None, st[0], st[1], vt, upper)

            carry, acc = lax.fori_loop(0, i, k_body, (carry, acc))
            _, acc = _sb_tile(_dot_nt(qh, k0), meta_key_q, carry, acc, v0, upper_c)
            res.append(acc)
        o_ref[0, pl.ds(r0, tq), :] = jnp.where(first_head(tq), res[0], res[1]).astype(o_ref.dtype)
        return 0

    lax.fori_loop(0, nq, q_body, 0)


def _sb_attention(u):
    bsz, p, _ = u.shape
    pairs = SB_WIDTH // LANE
    nq = (p - CHUNK) // SB_TILE
    assert nq * SB_TILE + CHUNK == p

    def ublock(base):
        return pl.BlockSpec((1, p, LANE), lambda b, hp: (b, 0, base + hp))

    return pl.pallas_call(
        functools.partial(_sb_kernel, nq=nq),
        grid=(bsz, pairs),
        in_specs=[ublock(0), ublock(pairs), ublock(2 * pairs)],
        out_specs=pl.BlockSpec((1, p, LANE), lambda b, hp: (b, 0, hp)),
        out_shape=jax.ShapeDtypeStruct((bsz, p, SB_WIDTH), BF16),
        compiler_params=_params(("parallel", "parallel"), 32),
        name="stickbreak_attention",
    )(u, u, u)


def _gelu_tanh(x):
    return 0.5 * x * (1.0 + jnp.tanh(math.sqrt(2.0 / math.pi) * (x + 0.044715 * (x * x * x))))


def _lru_kernel(gate_ref, xr_ref, cw_ref, cb_ref, wax_ref, ba_ref, bx_ref, lam_ref, o_ref, buf, hcar):
    t = pl.program_id(1)
    L = o_ref.shape[1]
    w = LRU_WIDTH

    @pl.when(t == 0)
    def _():
        buf[0:HALO, :] = jnp.zeros((HALO, w), F32)
        hcar[...] = jnp.zeros(hcar.shape, F32)

    x = _causal_conv(buf, xr_ref[0].astype(F32), cw_ref, cb_ref, LRU_CONV, L)
    x16 = x.astype(BF16)
    gates = [_dot(x16[:, n * LRU_BLOCK:(n + 1) * LRU_BLOCK], wax_ref[n]) for n in range(LRU_BLOCKS)]
    r = _sigmoid(jnp.concatenate([g[:, :LRU_BLOCK] for g in gates], axis=-1) + ba_ref[...])
    i = _sigmoid(jnp.concatenate([g[:, LRU_BLOCK:] for g in gates], axis=-1) + bx_ref[...])
    log_a = -LRU_C * r * _softplus(-lam_ref[...])
    a = jnp.exp(log_a)
    b = jnp.sqrt(jnp.maximum(1.0 - jnp.exp(2.0 * log_a), 0.0)) * (i * x)
    valid = (t * L + lax.broadcasted_iota(jnp.int32, (L, 1), 0)) >= PAD_ROWS
    b = jnp.where(valid, b, 0.0)

    row = lax.broadcasted_iota(jnp.int32, (L, 1), 0)
    k = 1
    while k < L:
        keep = row >= k
        a_sh = jnp.where(keep, pltpu.roll(a, k, 0), 1.0)
        b_sh = jnp.where(keep, pltpu.roll(b, k, 0), 0.0)
        b = a * b_sh + b
        a = a * a_sh
        k *= 2
    hs = a * hcar[...] + b
    hcar[...] = hs[L - 1:L, :]
    o_ref[0] = (hs * _gelu_tanh(gate_ref[0].astype(F32))).astype(o_ref.dtype)


def _rg_lru(u, conv_w, conv_b, w_ax, b_a, b_x, lam):
    bsz, p, _ = u.shape
    w = LRU_WIDTH
    tt = CHUNK

    def full(shape):
        return pl.BlockSpec(shape, lambda b, t: (0,) * len(shape))

    return pl.pallas_call(
        _lru_kernel,
        grid=(bsz, p // tt),
        in_specs=[
            pl.BlockSpec((1, tt, w), lambda b, t: (b, t, 3)),
            pl.BlockSpec((1, tt, w), lambda b, t: (b, t, 4)),
            full((LRU_CONV, w)), full((1, w)), full((LRU_BLOCKS, LRU_BLOCK, 2 * LRU_BLOCK)),
            full((1, w)), full((1, w)), full((1, w)),
        ],
        out_specs=pl.BlockSpec((1, tt, w), lambda b, t: (b, t, 0)),
        out_shape=jax.ShapeDtypeStruct((bsz, p, w), BF16),
        scratch_shapes=[pltpu.VMEM((HALO + tt, w), F32), pltpu.VMEM((1, w), F32)],
        compiler_params=_params(("parallel", "arbitrary"), 32),
        name="rg_lru",
    )(u, u, conv_w, conv_b.reshape(1, w), w_ax, b_a.reshape(1, w), b_x.reshape(1, w), lam.reshape(1, w))


def _final_norm_kernel(h_ref, g_ref, o_ref):
    x = h_ref[0, CHUNK:, :]
    ms = jnp.mean(x * x, axis=-1, keepdims=True)
    o_ref[0] = (x * lax.rsqrt(ms + EPS)) * g_ref[...]


def _final_norm(h, gain):
    bsz, p, d = h.shape
    return pl.pallas_call(
        _final_norm_kernel,
        grid=(bsz,),
        in_specs=[pl.BlockSpec((1, p, d), lambda b: (b, 0, 0)), pl.BlockSpec((1, d), lambda b: (0, 0))],
        out_specs=pl.BlockSpec((1, p - CHUNK, d), lambda b: (b, 0, 0)),
        out_shape=jax.ShapeDtypeStruct((bsz, p - CHUNK, d), F32),
        compiler_params=_params(("parallel",), 48),
        name="final_norm",
    )(h, gain.reshape(1, d))


def kernel(x, meta_tokens, l0_mix_norm, l0_w_in, l0_ssd_conv_w, l0_ssd_conv_b, l0_ssd_dt_bias, l0_ssd_a_log,
           l0_ssd_d, l0_ssd_norm, l0_ret_norm, l0_w_out, l0_ffn_norm, l0_ffn_w_in, l0_ffn_conv_w, l0_ffn_conv_b,
           l0_ffn_w_out, l1_mix_norm, l1_w_in, l1_lru_conv_w, l1_lru_conv_b, l1_lru_wa, l1_lru_ba, l1_lru_wx,
           l1_lru_bx, l1_lru_lambda, l1_w_out, l1_ffn_norm, l1_ffn_w_in, l1_ffn_conv_w, l1_ffn_conv_b,
           l1_ffn_w_out, final_norm):
    bsz, seq, d = x.shape
    p = seq + CHUNK
    assert d == D_MODEL and seq % SB_TILE == 0

    lead = jnp.concatenate([jnp.zeros((PAD_ROWS, d), x.dtype), meta_tokens.astype(x.dtype)], 0)
    h = jnp.concatenate([jnp.broadcast_to(lead[None], (bsz, CHUNK, d)), x], 1)

    dt0 = SSD_INNER + SSD_INNER + 2 * SSD_GROUPS * SSD_STATE
    w_main = jnp.concatenate([l0_w_in[:, :dt0], l0_w_in[:, dt0 + SSD_HEADS:]], 1).astype(BF16)
    w_dt = jnp.pad(l0_w_in[:, dt0:dt0 + SSD_HEADS], ((0, 0), (0, LANE - SSD_HEADS))).astype(BF16)
    u, dt_raw = _norm_matmul(h.reshape(bsz * p, d), l0_mix_norm, w_main, w_dt)
    u = u.reshape(bsz, p, -1)
    y_ssd = _ssd(u, dt_raw.reshape(bsz, p, LANE), l0_ssd_conv_w, l0_ssd_conv_b, l0_ssd_dt_bias, l0_ssd_a_log,
                 l0_ssd_d, l0_ssd_norm)
    y_ret = _retention(u, l0_ret_norm)
    w_out = l0_w_out.astype(BF16)
    h = _outproj(y_ssd, y_ret, w_out[:SSD_INNER], w_out[SSD_INNER:], h)
    h = _conv_ffn(h, l0_ffn_norm, l0_ffn_w_in.astype(BF16), l0_ffn_conv_w, l0_ffn_conv_b,
                  l0_ffn_w_out.astype(BF16))

    u = _norm_matmul(h.reshape(bsz * p, d), l1_mix_norm, l1_w_in.astype(BF16)).reshape(bsz, p, -1)
    y_sb = _sb_attention(u)
    w_ax = jnp.concatenate([l1_lru_wa, l1_lru_wx], -1).astype(BF16)
    y_lru = _rg_lru(u, l1_lru_conv_w, l1_lru_conv_b, w_ax, l1_lru_ba, l1_lru_bx, l1_lru_lambda)
    w_out = l1_w_out.astype(BF16)
    h = _outproj(y_sb, y_lru, w_out[:SB_WIDTH], w_out[SB_WIDTH:], h)
    h = _conv_ffn(h, l1_ffn_norm, l1_ffn_w_in.astype(BF16), l1_ffn_conv_w, l1_ffn_conv_b,
                  l1_ffn_w_out.astype(BF16))

    return _final_norm(h, final_norm)
```

```python
import functools
import math

import jax
import jax.numpy as jnp
from jax import lax
from jax.experimental import pallas as pl
from jax.experimental.pallas import tpu as pltpu

F32 = jnp.float32
BF16 = jnp.bfloat16

D_MODEL = 1024
N_META = 16
CHUNK = 128
PAD_ROWS = CHUNK - N_META
EPS = 1e-6

SSD_HEADS = 16
SSD_HEAD_DIM = 64
SSD_INNER = SSD_HEADS * SSD_HEAD_DIM
SSD_GROUPS = 4
SSD_STATE = 128
SSD_CONV = 4
GROUP_W = SSD_INNER // SSD_GROUPS
HEADS_PER_GROUP = SSD_HEADS // SSD_GROUPS

RET_HEADS = 4
RET_DIM = 256

SB_HEADS = 16
SB_HEAD_DIM = 64
SB_WIDTH = SB_HEADS * SB_HEAD_DIM
SB_TILE = 256

LRU_WIDTH = 1024
LRU_BLOCKS = 8
LRU_BLOCK = LRU_WIDTH // LRU_BLOCKS
LRU_CONV = 4
LRU_C = 8.0

FFN_DIM = 2816
FFN_CONV = 3
FFN_SPLIT = 2
FFN_CHUNK = FFN_DIM // FFN_SPLIT

LANE = 128
HALO = 8
MIB = 1024 * 1024


def _params(sem, vmem_mib):
    return pltpu.CompilerParams(dimension_semantics=sem, vmem_limit_bytes=vmem_mib * MIB)


def _row_tile(n, cap, mult=16):
    best = None
    for t in range(mult, min(n, cap) + 1, mult):
        if n % t == 0:
            best = t
    assert best is not None, (n, cap)
    return best


def _sigmoid(x):
    return 1.0 / (1.0 + jnp.exp(-x))


def _silu(x):
    return x * _sigmoid(x)


def _softplus(x):
    return jnp.maximum(x, 0.0) + jnp.log1p(jnp.exp(-jnp.abs(x)))


def _dot(a, b):
    return jnp.dot(a, b, preferred_element_type=F32)


def _dot_nt(a, b):
    return lax.dot_general(a, b, (((1,), (1,)), ((), ())), preferred_element_type=F32)


def _split3_dot(tri_bf16, x):
    x1 = x.astype(BF16)
    r1 = x - x1.astype(F32)
    x2 = r1.astype(BF16)
    x3 = (r1 - x2.astype(F32)).astype(BF16)
    return _dot(tri_bf16, x1) + _dot(tri_bf16, x2) + _dot(tri_bf16, x3)


def _norm_matmul_kernel(x_ref, g_ref, w_ref, *rest, has_aux):
    if has_aux:
        waux_ref, o_ref, aux_ref, xn_ref = rest
    else:
        o_ref, xn_ref = rest

    @pl.when(pl.program_id(1) == 0)
    def _():
        x = x_ref[...]
        ms = jnp.mean(x * x, axis=-1, keepdims=True)
        xn_ref[...] = ((x * lax.rsqrt(ms + EPS)) * g_ref[...]).astype(BF16)
        if has_aux:
            aux_ref[...] = _dot(xn_ref[...], waux_ref[...])

    o_ref[...] = _dot(xn_ref[...], w_ref[...]).astype(o_ref.dtype)


def _norm_matmul(h2d, gain, w, w_aux=None):
    m, d = h2d.shape
    n = w.shape[1]
    tm = _row_tile(m, 1024)
    tn = 1024
    assert n % tn == 0
    has_aux = w_aux is not None
    in_specs = [
        pl.BlockSpec((tm, d), lambda i, j: (i, 0)),
        pl.BlockSpec((1, d), lambda i, j: (0, 0)),
        pl.BlockSpec((d, tn), lambda i, j: (0, j)),
    ]
    out_shape = [jax.ShapeDtypeStruct((m, n), BF16)]
    out_specs = [pl.BlockSpec((tm, tn), lambda i, j: (i, j))]
    args = [h2d, gain.reshape(1, d), w]
    if has_aux:
        in_specs.append(pl.BlockSpec((d, LANE), lambda i, j: (0, 0)))
        out_shape.append(jax.ShapeDtypeStruct((m, LANE), F32))
        out_specs.append(pl.BlockSpec((tm, LANE), lambda i, j: (i, 0)))
        args.append(w_aux)
    outs = pl.pallas_call(
        functools.partial(_norm_matmul_kernel, has_aux=has_aux),
        grid=(m // tm, n // tn),
        in_specs=in_specs,
        out_specs=out_specs,
        out_shape=out_shape,
        scratch_shapes=[pltpu.VMEM((tm, d), BF16)],
        compiler_params=_params(("parallel", "arbitrary"), 40),
        name="norm_matmul_aux" if has_aux else "norm_matmul",
    )(*args)
    return outs if has_aux else outs[0]


def _valid_rows(chunk_idx, rows):
    r = lax.broadcasted_iota(jnp.int32, (rows, 1), 0)
    return jnp.logical_or(chunk_idx > 0, r >= PAD_ROWS)


def _causal_conv(buf_ref, raw, w_ref, b_ref, taps, rows):
    buf_ref[HALO:HALO + rows, :] = raw
    out = b_ref[...]
    for k in range(taps):
        off = HALO - (taps - 1) + k
        out = out + w_ref[k:k + 1, :] * buf_ref[off:off + rows, :]
    buf_ref[0:HALO, :] = raw[rows - HALO:rows, :]
    return out


def _rep_heads(cols, g):
    rows = cols.shape[0]
    lane = lax.broadcasted_iota(jnp.int32, (rows, GROUP_W), 1)
    h0 = g * HEADS_PER_GROUP
    out = jnp.broadcast_to(cols[:, h0 + 3:h0 + 4], (rows, GROUP_W))
    for hh in (2, 1, 0):
        out = jnp.where(lane < (hh + 1) * SSD_HEAD_DIM, cols[:, h0 + hh:h0 + hh + 1], out)
    return out


def _ssd_kernel(z_ref, xs_ref, bc_ref, dt_ref, cwx_ref, cwbc_ref, cbx_ref, cbbc_ref, dtb_ref, alog_ref,
                dskip_ref, ng_ref, o_ref, bufx, bufbc, state):
    c = pl.program_id(1)
    L = CHUNK

    @pl.when(c == 0)
    def _():
        bufx[0:HALO, :] = jnp.zeros((HALO, SSD_INNER), F32)
        bufbc[0:HALO, :] = jnp.zeros((HALO, 2 * SSD_GROUPS * SSD_STATE), F32)
        state[...] = jnp.zeros(state.shape, F32)

    valid = _valid_rows(c, L)
    xs = _causal_conv(bufx, xs_ref[0].astype(F32), cwx_ref, cbx_ref, SSD_CONV, L)
    xs = jnp.where(valid, _silu(xs), 0.0)
    bc = _causal_conv(bufbc, bc_ref[0].astype(F32), cwbc_ref, cbbc_ref, SSD_CONV, L)
    bc = jnp.where(valid, _silu(bc), 0.0)

    dt = jnp.where(valid, _softplus(dt_ref[0] + dtb_ref[...]), 0.0)
    a_dt = dt * (-jnp.exp(alog_ref[...]))
    row = lax.broadcasted_iota(jnp.int32, (L, L), 0)
    col = lax.broadcasted_iota(jnp.int32, (L, L), 1)
    lower = row >= col
    a_col = _split3_dot(lower.astype(BF16), a_dt)
    a_row = a_col.T
    lane = lax.broadcasted_iota(jnp.int32, (L, GROUP_W), 1)

    for g in range(SSD_GROUPS):
        sl = slice(g * GROUP_W, (g + 1) * GROUP_W)
        bm = bc[:, g * SSD_STATE:(g + 1) * SSD_STATE]
        cm = bc[:, (SSD_GROUPS + g) * SSD_STATE:(SSD_GROUPS + g + 1) * SSD_STATE]
        bm16 = bm.astype(BF16)
        cm16 = cm.astype(BF16)
        cb = _dot_nt(cm16, bm16)
        a_rep = _rep_heads(a_col, g)
        xg = xs[:, sl]
        xdt = xg * _rep_heads(dt, g)
        xdt16 = xdt.astype(BF16)
        y = jnp.zeros((L, GROUP_W), F32)
        for hh in range(HEADS_PER_GROUP):
            h = g * HEADS_PER_GROUP + hh
            seg = a_col[:, h:h + 1] - a_row[h:h + 1, :]
            decay = jnp.where(lower, jnp.exp(seg), 0.0)
            yh = _dot((cb * decay).astype(BF16), xdt16)
            y = jnp.where(jnp.logical_and(lane >= hh * SSD_HEAD_DIM, lane < (hh + 1) * SSD_HEAD_DIM), yh, y)
        s_prev = state[g]
        y = y + _dot(cm16, s_prev.astype(BF16)) * jnp.exp(a_rep)
        a_last = a_rep[L - 1:L, :]
        xdec = (xdt * jnp.exp(a_last - a_rep)).astype(BF16)
        state[g] = s_prev * jnp.exp(a_last) + _dot(bm.T.astype(BF16), xdec)
        y = y + xg * dskip_ref[:, sl]
        y = y * _silu(z_ref[0, :, sl].astype(F32))
        y = y * lax.rsqrt(jnp.mean(y * y, axis=-1, keepdims=True) + EPS)
        o_ref[0, :, sl] = (y * ng_ref[:, sl]).astype(o_ref.dtype)


def _ssd(u, dt_raw, conv_w, conv_b, dt_bias, a_log, d_skip, norm_g):
    bsz, p, _ = u.shape
    nc = p // CHUNK
    w = SSD_INNER

    def ublock(idx):
        return pl.BlockSpec((1, CHUNK, w), lambda b, c: (b, c, idx))

    def full(shape):
        return pl.BlockSpec(shape, lambda b, c: (0,) * len(shape))

    pad_heads = LANE - SSD_HEADS
    args = (
        u, u, u, dt_raw,
        conv_w[:, :w], conv_w[:, w:], conv_b[:w].reshape(1, w), conv_b[w:].reshape(1, w),
        jnp.pad(dt_bias, (0, pad_heads)).reshape(1, LANE), jnp.pad(a_log, (0, pad_heads)).reshape(1, LANE),
        jnp.repeat(d_skip, SSD_HEAD_DIM).reshape(1, w), norm_g.reshape(1, w),
    )
    in_specs = [
        ublock(0), ublock(1), ublock(2),
        pl.BlockSpec((1, CHUNK, LANE), lambda b, c: (b, c, 0)),
        full((SSD_CONV, w)), full((SSD_CONV, w)), full((1, w)), full((1, w)),
        full((1, LANE)), full((1, LANE)), full((1, w)), full((1, w)),
    ]
    return pl.pallas_call(
        _ssd_kernel,
        grid=(bsz, nc),
        in_specs=in_specs,
        out_specs=pl.BlockSpec((1, CHUNK, w), lambda b, c: (b, c, 0)),
        out_shape=jax.ShapeDtypeStruct((bsz, p, w), BF16),
        scratch_shapes=[
            pltpu.VMEM((HALO + CHUNK, w), F32),
            pltpu.VMEM((HALO + CHUNK, w), F32),
            pltpu.VMEM((SSD_GROUPS, SSD_STATE, GROUP_W), F32),
        ],
        compiler_params=_params(("parallel", "arbitrary"), 32),
        name="ssd_scan",
    )(*args)


def _rotate(x, cos, sin):
    half = RET_DIM // 2
    x1, x2 = x[:, :half], x[:, half:]
    return jnp.concatenate([x1 * cos - x2 * sin, x1 * sin + x2 * cos], axis=-1)


def _ret_kernel(q_ref, k_ref, v_ref, g_ref, cos_ref, sin_ref, dec_ref, zeta_ref, xi_ref, cd_ref, ng_ref,
                o_ref, state):
    c = pl.program_id(1)
    L = CHUNK

    @pl.when(c == 0)
    def _():
        state[...] = jnp.zeros(state.shape, F32)

    valid = _valid_rows(c, L)
    cos = cos_ref[...]
    sin = sin_ref[...]
    for h in range(RET_HEADS):
        sl = slice(h * RET_DIM, (h + 1) * RET_DIM)
        q = jnp.where(valid, _rotate(q_ref[0, :, sl].astype(F32), cos, sin), 0.0)
        k = jnp.where(valid, _rotate(k_ref[0, :, sl].astype(F32), cos, sin) * (RET_DIM ** -0.5), 0.0)
        v16 = jnp.where(valid, v_ref[0, :, sl], jnp.zeros((), BF16))
        q16 = q.astype(BF16)
        scores = _dot_nt(q16, k.astype(BF16)) * dec_ref[h]
        r_prev = state[h]
        o = _dot(scores.astype(BF16), v16) + _dot(q16, r_prev.astype(BF16)) * xi_ref[:, h:h + 1]
        kz = (k * zeta_ref[:, h:h + 1]).T.astype(BF16)
        state[h] = r_prev * cd_ref[:, h:h + 1] + _dot(kz, v16)
        o = o - jnp.mean(o, axis=-1, keepdims=True)
        o = o * lax.rsqrt(jnp.mean(o * o, axis=-1, keepdims=True) + EPS)
        o = o * ng_ref[:, sl]
        o_ref[0, :, sl] = (_silu(g_ref[0, :, sl].astype(F32)) * o).astype(o_ref.dtype)


def _retention(u, norm_g):
    bsz, p, _ = u.shape
    nc = p // CHUNK
    w = RET_HEADS * RET_DIM
    half = RET_DIM // 2
    pos = jnp.arange(p, dtype=F32) - PAD_ROWS
    inv_freq = 1.0 / (10000.0 ** (jnp.arange(half, dtype=F32) / (half - 1)))
    ang = pos[:, None] * inv_freq[None, :]
    log_gamma = jnp.log1p(-jnp.exp2(-5.0 - jnp.arange(RET_HEADS, dtype=F32)))
    idx = jnp.arange(CHUNK, dtype=F32)
    diff = idx[:, None] - idx[None, :]
    decay = jnp.where(diff >= 0, jnp.exp(log_gamma[:, None, None] * jnp.maximum(diff, 0.0)), 0.0)
    pad_heads = LANE - RET_HEADS
    zeta = jnp.pad(jnp.exp(log_gamma[None, :] * (CHUNK - 1 - idx)[:, None]), ((0, 0), (0, pad_heads)))
    xi = jnp.pad(jnp.exp(log_gamma[None, :] * (idx + 1.0)[:, None]), ((0, 0), (0, pad_heads)))
    chunk_decay = jnp.pad(jnp.exp(CHUNK * log_gamma), (0, pad_heads)).reshape(1, LANE)

    def ublock(idx_):
        return pl.BlockSpec((1, CHUNK, w), lambda b, c: (b, c, idx_))

    def full(shape):
        return pl.BlockSpec(shape, lambda b, c: (0,) * len(shape))

    return pl.pallas_call(
        _ret_kernel,
        grid=(bsz, nc),
        in_specs=[
            ublock(3), ublock(4), ublock(5), ublock(6),
            pl.BlockSpec((CHUNK, half), lambda b, c: (c, 0)),
            pl.BlockSpec((CHUNK, half), lambda b, c: (c, 0)),
            full((RET_HEADS, CHUNK, CHUNK)), full((CHUNK, LANE)), full((CHUNK, LANE)), full((1, LANE)),
            full((1, w)),
        ],
        out_specs=pl.BlockSpec((1, CHUNK, w), lambda b, c: (b, c, 0)),
        out_shape=jax.ShapeDtypeStruct((bsz, p, w), BF16),
        scratch_shapes=[pltpu.VMEM((RET_HEADS, RET_DIM, RET_DIM), F32)],
        compiler_params=_params(("parallel", "arbitrary"), 32),
        name="retention",
    )(u, u, u, u, jnp.cos(ang), jnp.sin(ang), decay, zeta, xi, chunk_decay, norm_g.reshape(1, w))


def _outproj_kernel(y1_ref, y2_ref, w1_ref, w2_ref, h_ref, o_ref):
    tm = o_ref.shape[1]
    acc = h_ref[0] + _dot(y1_ref[0], w1_ref[...]) + _dot(y2_ref[0], w2_ref[...])
    r = pl.program_id(1) * tm + lax.broadcasted_iota(jnp.int32, (tm, 1), 0)
    o_ref[0] = jnp.where(r >= PAD_ROWS, acc, 0.0)


def _outproj(y1, y2, w1, w2, h):
    bsz, p, d = h.shape
    tm = _row_tile(p, 544)
    k1, k2 = y1.shape[-1], y2.shape[-1]
    return pl.pallas_call(
        _outproj_kernel,
        grid=(bsz, p // tm),
        in_specs=[
            pl.BlockSpec((1, tm, k1), lambda b, t: (b, t, 0)),
            pl.BlockSpec((1, tm, k2), lambda b, t: (b, t, 0)),
            pl.BlockSpec((k1, d), lambda b, t: (0, 0)),
            pl.BlockSpec((k2, d), lambda b, t: (0, 0)),
            pl.BlockSpec((1, tm, d), lambda b, t: (b, t, 0)),
        ],
        out_specs=pl.BlockSpec((1, tm, d), lambda b, t: (b, t, 0)),
        out_shape=jax.ShapeDtypeStruct((bsz, p, d), F32),
        compiler_params=_params(("parallel", "parallel"), 40),
        name="outproj_residual",
    )(y1, y2, w1, w2, h)


def _ffn_kernel(h_ref, gain_ref, wg_ref, wu_ref, cwg_ref, cwu_ref, cbg_ref, cbu_ref, wo_ref, o_ref,
                xn, acc, bufg, bufu):
    s = pl.program_id(1)
    t = pl.program_id(2)
    tm = o_ref.shape[1]
    rows = pl.ds(pl.multiple_of(t * tm, tm), tm)

    @pl.when(s == 0)
    def _():
        x = h_ref[0]
        ms = jnp.mean(x * x, axis=-1, keepdims=True)
        xn[rows, :] = ((x * lax.rsqrt(ms + EPS)) * gain_ref[...]).astype(BF16)
        acc[rows, :] = x

    @pl.when(t == 0)
    def _():
        bufg[0:HALO, :] = jnp.zeros((HALO, FFN_CHUNK), F32)
        bufu[0:HALO, :] = jnp.zeros((HALO, FFN_CHUNK), F32)

    xt = xn[rows, :]
    gate = _causal_conv(bufg, _dot(xt, wg_ref[...]), cwg_ref, cbg_ref, FFN_CONV, tm)
    up = _causal_conv(bufu, _dot(xt, wu_ref[...]), cwu_ref, cbu_ref, FFN_CONV, tm)
    act = (_silu(gate) * up).astype(BF16)
    acc[rows, :] += _dot(act, wo_ref[...])

    @pl.when(s == FFN_SPLIT - 1)
    def _():
        r = t * tm + lax.broadcasted_iota(jnp.int32, (tm, 1), 0)
        o_ref[0] = jnp.where(r >= PAD_ROWS, acc[rows, :], 0.0)


def _conv_ffn(h, gain, w_in, conv_w, conv_b, w_out):
    bsz, p, d = h.shape
    tm = _row_tile(p, 272)
    nt = p // tm
    fc = FFN_CHUNK
    cb = conv_b.reshape(1, 2 * FFN_DIM)
    return pl.pallas_call(
        _ffn_kernel,
        grid=(bsz, FFN_SPLIT, nt),
        in_specs=[
            pl.BlockSpec((1, tm, d), lambda b, s, t: (b, t * (1 - s), 0)),
            pl.BlockSpec((1, d), lambda b, s, t: (0, 0)),
            pl.BlockSpec((d, fc), lambda b, s, t: (0, s)),
            pl.BlockSpec((d, fc), lambda b, s, t: (0, FFN_SPLIT + s)),
            pl.BlockSpec((FFN_CONV, fc), lambda b, s, t: (0, s)),
            pl.BlockSpec((FFN_CONV, fc), lambda b, s, t: (0, FFN_SPLIT + s)),
            pl.BlockSpec((1, fc), lambda b, s, t: (0, s)),
            pl.BlockSpec((1, fc), lambda b, s, t: (0, FFN_SPLIT + s)),
            pl.BlockSpec((fc, d), lambda b, s, t: (s, 0)),
        ],
        out_specs=pl.BlockSpec((1, tm, d), lambda b, s, t: (b, t * (s // (FFN_SPLIT - 1)), 0)),
        out_shape=jax.ShapeDtypeStruct((bsz, p, d), F32),
        scratch_shapes=[
            pltpu.VMEM((p, d), BF16),
            pltpu.VMEM((p, d), F32),
            pltpu.VMEM((HALO + tm, fc), F32),
            pltpu.VMEM((HALO + tm, fc), F32),
        ],
        compiler_params=_params(("parallel", "arbitrary", "arbitrary"), 56),
        name="conv_ffn",
    )(h, gain.reshape(1, d), w_in, w_in, conv_w, conv_w, cb, cb, w_out)


def _sb_tiles(items):
    zs = [_dot_nt(q, k) for q, k, _, _, _ in items]
    log1ms, hls = [], []
    for z, (_, _, _, mask, _) in zip(zs, items):
        log1m = -(jnp.maximum(z, 0.0) + jnp.log(1.0 + jnp.exp(-jnp.abs(z))))
        if mask is not None:
            log1m = jnp.where(mask, log1m, 0.0)
        hi = log1m.astype(BF16)
        lo = (log1m - hi.astype(F32)).astype(BF16)
        log1ms.append(log1m)
        hls.append(jnp.concatenate([hi, lo], axis=1))
    afters = [_dot(hl, item[4]) for hl, item in zip(hls, items)]
    ws = []
    for z, log1m, after, (_, _, _, mask, _) in zip(zs, log1ms, afters, items):
        w = jnp.exp(z + log1m + after)
        if mask is not None:
            w = jnp.where(mask, w, 0.0)
        ws.append(w.astype(BF16))
    pvs = [_dot(w, item[2]) for w, item in zip(ws, items)]
    return [(pv, jnp.broadcast_to(after[:, 0:1] + log1m[:, 0:1], pv.shape))
            for pv, after, log1m in zip(pvs, afters, log1ms)]


def _sb_kernel(q_ref, k_ref, v_ref, o_ref, acc_ref, car_ref, pvm_ref, *, nq):
    tq = SB_TILE

    def iotas(rows, cols):
        return lax.broadcasted_iota(jnp.int32, (rows, cols), 0), lax.broadcasted_iota(jnp.int32, (rows, cols), 1)

    def stacked_upper(n):
        r, c = iotas(2 * n, n)
        return (jnp.where(r >= n, r - n, r) > c).astype(BF16)

    upper2 = stacked_upper(tq)
    upper2_c = stacked_upper(CHUNK)
    rr, cc = iotas(tq, tq)
    strict = cc < rr
    rr_c, cc_c = iotas(CHUNK, CHUNK)
    scale = SB_HEAD_DIM ** -0.5

    def first_head(rows):
        return lax.broadcasted_iota(jnp.int32, (rows, LANE), 1) < SB_HEAD_DIM

    def head_q(q16, hd):
        keep = first_head(q16.shape[0])
        if hd == 1:
            keep = jnp.logical_not(keep)
        return jnp.where(keep, q16.astype(F32) * scale, 0.0).astype(BF16)

    q0 = q_ref[0, 0:CHUNK, :]
    k0 = k_ref[0, 0:CHUNK, :]
    v0 = v_ref[0, 0:CHUNK, :]
    meta_mask = jnp.logical_and(cc_c < rr_c, cc_c >= PAD_ROWS)
    outs = _sb_tiles([(head_q(q0, hd), k0, v0, meta_mask, upper2_c) for hd in range(2)])
    o_ref[0, 0:CHUNK, :] = jnp.where(first_head(CHUNK), outs[0][0], outs[1][0]).astype(o_ref.dtype)

    meta_key_q = lax.broadcasted_iota(jnp.int32, (tq, CHUNK), 1) >= PAD_ROWS

    def q_body(i, _):
        r0 = pl.multiple_of(CHUNK + i * tq, CHUNK)
        q16 = q_ref[0, pl.ds(r0, tq), :]
        qs = [head_q(q16, hd) for hd in range(2)]
        kd = k_ref[0, pl.ds(r0, tq), :]
        vd = v_ref[0, pl.ds(r0, tq), :]
        first = _sb_tiles([(qs[hd], kd, vd, strict, upper2) for hd in range(2)]
                          + [(qs[hd], k0, v0, meta_key_q, upper2_c) for hd in range(2)])
        for hd in range(2):
            acc_ref[hd], car_ref[hd] = first[hd]
            pvm_ref[hd] = first[2 + hd][0]

        def visit(tiles):
            kv = []
            for j in tiles:
                rk = pl.multiple_of(CHUNK + j * tq, CHUNK)
                kv.append((k_ref[0, pl.ds(rk, tq), :], v_ref[0, pl.ds(rk, tq), :]))
            allparts = _sb_tiles([(qs[hd], kt, vt, None, upper2) for hd in range(2) for kt, vt in kv])
            for hd in range(2):
                parts = allparts[hd * len(kv):(hd + 1) * len(kv)]
                acc, car = acc_ref[hd], car_ref[hd]
                for pv, rowsum in parts:
                    acc = acc + pv * jnp.exp(car)
                    car = car + rowsum
                acc_ref[hd], car_ref[hd] = acc, car

        n_pairs = lax.shift_right_logical(i, 1)

        @pl.when(jnp.bitwise_and(i, 1) == 1)
        def _():
            visit([i - 1])

        def pair_body(n, _):
            j = 2 * (n_pairs - n) - 1
            visit([j, j - 1])
            return 0

        lax.fori_loop(0, n_pairs, pair_body, 0)
        res = [acc_ref[hd] + pvm_ref[hd] * jnp.exp(car_ref[hd]) for hd in range(2)]
        o_ref[0, pl.ds(r0, tq), :] = jnp.where(first_head(tq), res[0], res[1]).astype(o_ref.dtype)
        return 0

    lax.fori_loop(0, nq, q_body, 0)


def _sb_attention(u):
    bsz, p, _ = u.shape
    pairs = SB_WIDTH // LANE
    nq = (p - CHUNK) // SB_TILE
    assert nq * SB_TILE + CHUNK == p

    def ublock(base):
        return pl.BlockSpec((1, p, LANE), lambda b, hp: (b, 0, base + hp))

    return pl.pallas_call(
        functools.partial(_sb_kernel, nq=nq),
        grid=(bsz, pairs),
        in_specs=[ublock(0), ublock(pairs), ublock(2 * pairs)],
        out_specs=pl.BlockSpec((1, p, LANE), lambda b, hp: (b, 0, hp)),
        out_shape=jax.ShapeDtypeStruct((bsz, p, SB_WIDTH), BF16),
        scratch_shapes=[pltpu.VMEM((2, SB_TILE, LANE), F32)] * 3,
        compiler_params=_params(("parallel", "parallel"), 32),
        name="stickbreak_attention",
    )(u, u, u)


def _gelu_tanh(x):
    return 0.5 * x * (1.0 + jnp.tanh(math.sqrt(2.0 / math.pi) * (x + 0.044715 * (x * x * x))))


def _lru_kernel(gate_ref, xr_ref, cw_ref, cb_ref, wax_ref, ba_ref, bx_ref, lam_ref, o_ref, buf, hcar):
    t = pl.program_id(1)
    L = o_ref.shape[1]
    w = LRU_WIDTH

    @pl.when(t == 0)
    def _():
        buf[0:HALO, :] = jnp.zeros((HALO, w), F32)
        hcar[...] = jnp.zeros(hcar.shape, F32)

    x = _causal_conv(buf, xr_ref[0].astype(F32), cw_ref, cb_ref, LRU_CONV, L)
    x16 = x.astype(BF16)
    gates = [_dot(x16[:, n * LRU_BLOCK:(n + 1) * LRU_BLOCK], wax_ref[n]) for n in range(LRU_BLOCKS)]
    r = _sigmoid(jnp.concatenate([g[:, :LRU_BLOCK] for g in gates], axis=-1) + ba_ref[...])
    i = _sigmoid(jnp.concatenate([g[:, LRU_BLOCK:] for g in gates], axis=-1) + bx_ref[...])
    log_a = -LRU_C * r * _softplus(-lam_ref[...])
    a = jnp.exp(log_a)
    b = jnp.sqrt(jnp.maximum(1.0 - jnp.exp(2.0 * log_a), 0.0)) * (i * x)
    valid = (t * L + lax.broadcasted_iota(jnp.int32, (L, 1), 0)) >= PAD_ROWS
    b = jnp.where(valid, b, 0.0)

    row = lax.broadcasted_iota(jnp.int32, (L, 1), 0)
    k = 1
    while k < L:
        keep = row >= k
        a_sh = jnp.where(keep, pltpu.roll(a, k, 0), 1.0)
        b_sh = jnp.where(keep, pltpu.roll(b, k, 0), 0.0)
        b = a * b_sh + b
        a = a * a_sh
        k *= 2
    hs = a * hcar[...] + b
    hcar[...] = hs[L - 1:L, :]
    o_ref[0] = (hs * _gelu_tanh(gate_ref[0].astype(F32))).astype(o_ref.dtype)


def _rg_lru(u, conv_w, conv_b, w_ax, b_a, b_x, lam):
    bsz, p, _ = u.shape
    w = LRU_WIDTH
    tt = CHUNK

    def full(shape):
        return pl.BlockSpec(shape, lambda b, t: (0,) * len(shape))

    return pl.pallas_call(
        _lru_kernel,
        grid=(bsz, p // tt),
        in_specs=[
            pl.BlockSpec((1, tt, w), lambda b, t: (b, t, 3)),
            pl.BlockSpec((1, tt, w), lambda b, t: (b, t, 4)),
            full((LRU_CONV, w)), full((1, w)), full((LRU_BLOCKS, LRU_BLOCK, 2 * LRU_BLOCK)),
            full((1, w)), full((1, w)), full((1, w)),
        ],
        out_specs=pl.BlockSpec((1, tt, w), lambda b, t: (b, t, 0)),
        out_shape=jax.ShapeDtypeStruct((bsz, p, w), BF16),
        scratch_shapes=[pltpu.VMEM((HALO + tt, w), F32), pltpu.VMEM((1, w), F32)],
        compiler_params=_params(("parallel", "arbitrary"), 32),
        name="rg_lru",
    )(u, u, conv_w, conv_b.reshape(1, w), w_ax, b_a.reshape(1, w), b_x.reshape(1, w), lam.reshape(1, w))


def _final_norm_kernel(h_ref, g_ref, o_ref):
    x = h_ref[0, CHUNK:, :]
    ms = jnp.mean(x * x, axis=-1, keepdims=True)
    o_ref[0] = (x * lax.rsqrt(ms + EPS)) * g_ref[...]


def _final_norm(h, gain):
    bsz, p, d = h.shape
    return pl.pallas_call(
        _final_norm_kernel,
        grid=(bsz,),
        in_specs=[pl.BlockSpec((1, p, d), lambda b: (b, 0, 0)), pl.BlockSpec((1, d), lambda b: (0, 0))],
        out_specs=pl.BlockSpec((1, p - CHUNK, d), lambda b: (b, 0, 0)),
        out_shape=jax.ShapeDtypeStruct((bsz, p - CHUNK, d), F32),
        compiler_params=_params(("parallel",), 48),
        name="final_norm",
    )(h, gain.reshape(1, d))


def kernel(x, meta_tokens, l0_mix_norm, l0_w_in, l0_ssd_conv_w, l0_ssd_conv_b, l0_ssd_dt_bias, l0_ssd_a_log,
           l0_ssd_d, l0_ssd_norm, l0_ret_norm, l0_w_out, l0_ffn_norm, l0_ffn_w_in, l0_ffn_conv_w, l0_ffn_conv_b,
           l0_ffn_w_out, l1_mix_norm, l1_w_in, l1_lru_conv_w, l1_lru_conv_b, l1_lru_wa, l1_lru_ba, l1_lru_wx,
           l1_lru_bx, l1_lru_lambda, l1_w_out, l1_ffn_norm, l1_ffn_w_in, l1_ffn_conv_w, l1_ffn_conv_b,
           l1_ffn_w_out, final_norm):
    bsz, seq, d = x.shape
    p = seq + CHUNK
    assert d == D_MODEL and seq % SB_TILE == 0

    lead = jnp.concatenate([jnp.zeros((PAD_ROWS, d), x.dtype), meta_tokens.astype(x.dtype)], 0)
    h = jnp.concatenate([jnp.broadcast_to(lead[None], (bsz, CHUNK, d)), x], 1)

    dt0 = SSD_INNER + SSD_INNER + 2 * SSD_GROUPS * SSD_STATE
    w_main = jnp.concatenate([l0_w_in[:, :dt0], l0_w_in[:, dt0 + SSD_HEADS:]], 1).astype(BF16)
    w_dt = jnp.pad(l0_w_in[:, dt0:dt0 + SSD_HEADS], ((0, 0), (0, LANE - SSD_HEADS))).astype(BF16)
    u, dt_raw = _norm_matmul(h.reshape(bsz * p, d), l0_mix_norm, w_main, w_dt)
    u = u.reshape(bsz, p, -1)
    y_ssd = _ssd(u, dt_raw.reshape(bsz, p, LANE), l0_ssd_conv_w, l0_ssd_conv_b, l0_ssd_dt_bias, l0_ssd_a_log,
                 l0_ssd_d, l0_ssd_norm)
    y_ret = _retention(u, l0_ret_norm)
    w_out = l0_w_out.astype(BF16)
    h = _outproj(y_ssd, y_ret, w_out[:SSD_INNER], w_out[SSD_INNER:], h)
    h = _conv_ffn(h, l0_ffn_norm, l0_ffn_w_in.astype(BF16), l0_ffn_conv_w, l0_ffn_conv_b,
                  l0_ffn_w_out.astype(BF16))

    u = _norm_matmul(h.reshape(bsz * p, d), l1_mix_norm, l1_w_in.astype(BF16)).reshape(bsz, p, -1)
    y_sb = _sb_attention(u)
    w_ax = jnp.concatenate([l1_lru_wa, l1_lru_wx], -1).astype(BF16)
    y_lru = _rg_lru(u, l1_lru_conv_w, l1_lru_conv_b, w_ax, l1_lru_ba, l1_lru_bx, l1_lru_lambda)
    w_out = l1_w_out.astype(BF16)
    h = _outproj(y_sb, y_lru, w_out[:SB_WIDTH], w_out[SB_WIDTH:], h)
    h = _conv_ffn(h, l1_ffn_norm, l1_ffn_w_in.astype(BF16), l1_ffn_conv_w, l1_ffn_conv_b,
                  l1_ffn_w_out.astype(BF16))

    return _final_norm(h, final_norm)
```

```python
import functools
import math

import jax
import jax.numpy as jnp
from jax import lax
from jax.experimental import pallas as pl
from jax.experimental.pallas import tpu as pltpu

F32 = jnp.float32
BF16 = jnp.bfloat16

D_MODEL = 1024
N_META = 16
CHUNK = 128
PAD_ROWS = CHUNK - N_META
EPS = 1e-6

SSD_HEADS = 16
SSD_HEAD_DIM = 64
SSD_INNER = SSD_HEADS * SSD_HEAD_DIM
SSD_GROUPS = 4
SSD_STATE = 128
SSD_CONV = 4
GROUP_W = SSD_INNER // SSD_GROUPS
HEADS_PER_GROUP = SSD_HEADS // SSD_GROUPS

RET_HEADS = 4
RET_DIM = 256

SB_HEADS = 16
SB_HEAD_DIM = 64
SB_WIDTH = SB_HEADS * SB_HEAD_DIM
SB_TILE = 256

LRU_WIDTH = 1024
LRU_BLOCKS = 8
LRU_BLOCK = LRU_WIDTH // LRU_BLOCKS
LRU_CONV = 4
LRU_C = 8.0

FFN_DIM = 2816
FFN_CONV = 3
FFN_SUB_ROWS = 272

LANE = 128
HALO = 8
MIB = 1024 * 1024
LOG2E = 1.4426950408889634


def _params(sem, vmem_mib):
    return pltpu.CompilerParams(dimension_semantics=sem, vmem_limit_bytes=vmem_mib * MIB)


def _row_tile(n, cap, mult=16):
    best = None
    for t in range(mult, min(n, cap) + 1, mult):
        if n % t == 0:
            best = t
    assert best is not None, (n, cap)
    return best


def _sigmoid(x):
    return 1.0 / (1.0 + jnp.exp(-x))


def _silu(x):
    return x * _sigmoid(x)


def _softplus(x):
    return jnp.maximum(x, 0.0) + jnp.log1p(jnp.exp(-jnp.abs(x)))


def _dot(a, b):
    return jnp.dot(a, b, preferred_element_type=F32)


def _dot_nt(a, b):
    return lax.dot_general(a, b, (((1,), (1,)), ((), ())), preferred_element_type=F32)


def _split3_dot(tri_bf16, x):
    x1 = x.astype(BF16)
    r1 = x - x1.astype(F32)
    x2 = r1.astype(BF16)
    x3 = (r1 - x2.astype(F32)).astype(BF16)
    return _dot(tri_bf16, x1) + _dot(tri_bf16, x2) + _dot(tri_bf16, x3)


def _norm_matmul_kernel(x_ref, g_ref, w_ref, *rest, has_aux, tn):
    if has_aux:
        waux_ref, o_ref, aux_ref = rest
    else:
        (o_ref,) = rest
    x = x_ref[...]
    ms = jnp.mean(x * x, axis=-1, keepdims=True)
    xn = ((x * lax.rsqrt(ms + EPS)) * g_ref[...]).astype(BF16)
    if has_aux:
        aux_ref[...] = _dot(xn, waux_ref[...])
    for j in range(o_ref.shape[1] // tn):
        o_ref[:, j * tn:(j + 1) * tn] = _dot(xn, w_ref[:, j * tn:(j + 1) * tn]).astype(o_ref.dtype)


def _resident(shape):
    return pl.BlockSpec(shape, lambda *_: (0,) * len(shape), pipeline_mode=pl.Buffered(1))


def _norm_matmul(h2d, gain, w, w_aux=None):
    m, d = h2d.shape
    n = w.shape[1]
    tm = _row_tile(m, 512)
    tn = 1024
    assert n % tn == 0
    has_aux = w_aux is not None
    in_specs = [pl.BlockSpec((tm, d), lambda i: (i, 0)), _resident((1, d)), _resident((d, n))]
    out_shape = [jax.ShapeDtypeStruct((m, n), BF16)]
    out_specs = [pl.BlockSpec((tm, n), lambda i: (i, 0))]
    args = [h2d, gain.reshape(1, d), w]
    if has_aux:
        in_specs.append(_resident((d, LANE)))
        out_shape.append(jax.ShapeDtypeStruct((m, LANE), F32))
        out_specs.append(pl.BlockSpec((tm, LANE), lambda i: (i, 0)))
        args.append(w_aux)
    outs = pl.pallas_call(
        functools.partial(_norm_matmul_kernel, has_aux=has_aux, tn=tn),
        grid=(m // tm,),
        in_specs=in_specs,
        out_specs=out_specs,
        out_shape=out_shape,
        compiler_params=_params(("parallel",), 48),
        name="norm_matmul_aux" if has_aux else "norm_matmul",
    )(*args)
    return outs if has_aux else outs[0]


def _valid_rows(chunk_idx, rows):
    r = lax.broadcasted_iota(jnp.int32, (rows, 1), 0)
    return jnp.logical_or(chunk_idx > 0, r >= PAD_ROWS)


def _causal_conv(hist_ref, cur16, w_ref, b_ref, taps):
    rows = cur16.shape[0]
    hist_ref[rows:2 * rows, :] = cur16
    both = hist_ref[...]
    t = lax.broadcasted_iota(jnp.int32, (rows, 2 * rows), 0)
    j = lax.broadcasted_iota(jnp.int32, (rows, 2 * rows), 1)
    out = b_ref[...] + w_ref[taps - 1:taps, :] * cur16.astype(F32)
    for k in range(taps - 1):
        back = taps - 1 - k
        shift = (j == t + (rows - back)).astype(BF16)
        out = out + w_ref[k:k + 1, :] * _dot(shift, both)
    hist_ref[0:rows, :] = cur16
    return out


def _rep_heads(cols, g):
    rows = cols.shape[0]
    lane = lax.broadcasted_iota(jnp.int32, (rows, GROUP_W), 1)
    h0 = g * HEADS_PER_GROUP
    out = jnp.broadcast_to(cols[:, h0 + 3:h0 + 4], (rows, GROUP_W))
    for hh in (2, 1, 0):
        out = jnp.where(lane < (hh + 1) * SSD_HEAD_DIM, cols[:, h0 + hh:h0 + hh + 1], out)
    return out


def _ssd_kernel(z_ref, xs_ref, bc_ref, dt_ref, cw_ref, cb_ref, dtb_ref, alog_ref, dskip_ref, ng_ref, o_ref,
                hist, state):
    c = pl.program_id(1)
    L = CHUNK

    @pl.when(c == 0)
    def _():
        hist[0:L, :] = jnp.zeros((L, hist.shape[1]), BF16)
        state[...] = jnp.zeros(state.shape, F32)

    valid = _valid_rows(c, L)
    xbc = _causal_conv(hist, jnp.concatenate([xs_ref[0], bc_ref[0]], axis=-1), cw_ref, cb_ref, SSD_CONV)
    xbc = jnp.where(valid, _silu(xbc), 0.0)
    xs = xbc[:, :SSD_INNER]
    bc = xbc[:, SSD_INNER:]

    dt = jnp.where(valid, _softplus(dt_ref[0] + dtb_ref[...]), 0.0)
    a_dt = dt * (-jnp.exp(alog_ref[...]))
    row = lax.broadcasted_iota(jnp.int32, (L, L), 0)
    col = lax.broadcasted_iota(jnp.int32, (L, L), 1)
    lower = row >= col
    a_col = _split3_dot(lower.astype(BF16), a_dt)
    a_row = a_col.T
    lane = lax.broadcasted_iota(jnp.int32, (L, GROUP_W), 1)
    groups = range(SSD_GROUPS)

    bm = [bc[:, g * SSD_STATE:(g + 1) * SSD_STATE] for g in groups]
    cm16 = [bc[:, (SSD_GROUPS + g) * SSD_STATE:(SSD_GROUPS + g + 1) * SSD_STATE].astype(BF16) for g in groups]
    s_prev = [state[g] for g in groups]
    cb = [_dot_nt(cm16[g], bm[g].astype(BF16)) for g in groups]
    y_off = [_dot(cm16[g], s_prev[g].astype(BF16)) for g in groups]
    a_rep = [_rep_heads(a_col, g) for g in groups]
    xg = [xs[:, g * GROUP_W:(g + 1) * GROUP_W] for g in groups]
    xdt = [xg[g] * _rep_heads(dt, g) for g in groups]
    xdt16 = [x.astype(BF16) for x in xdt]
    scores = []
    for h in range(SSD_HEADS):
        seg = a_col[:, h:h + 1] - a_row[h:h + 1, :]
        decay = jnp.where(lower, jnp.exp(seg), 0.0)
        scores.append((cb[h // HEADS_PER_GROUP] * decay).astype(BF16))
    y_diag = [_dot(scores[h], xdt16[h // HEADS_PER_GROUP]) for h in range(SSD_HEADS)]
    a_last = [a[L - 1:L, :] for a in a_rep]
    xdec = [(xdt[g] * jnp.exp(a_last[g] - a_rep[g])).astype(BF16) for g in groups]
    s_new = [_dot(bm[g].T.astype(BF16), xdec[g]) for g in groups]
    for g in groups:
        sl = slice(g * GROUP_W, (g + 1) * GROUP_W)
        state[g] = s_prev[g] * jnp.exp(a_last[g]) + s_new[g]
        y = y_off[g] * jnp.exp(a_rep[g])
        for hh in range(HEADS_PER_GROUP):
            own = jnp.logical_and(lane >= hh * SSD_HEAD_DIM, lane < (hh + 1) * SSD_HEAD_DIM)
            y = y + jnp.where(own, y_diag[g * HEADS_PER_GROUP + hh], 0.0)
        y = y + xg[g] * dskip_ref[:, sl]
        y = y * _silu(z_ref[0, :, sl].astype(F32))
        y = y * lax.rsqrt(jnp.mean(y * y, axis=-1, keepdims=True) + EPS)
        o_ref[0, :, sl] = (y * ng_ref[:, sl]).astype(o_ref.dtype)


def _ssd(u, dt_raw, conv_w, conv_b, dt_bias, a_log, d_skip, norm_g):
    bsz, p, _ = u.shape
    nc = p // CHUNK
    w = SSD_INNER

    def ublock(idx):
        return pl.BlockSpec((1, CHUNK, w), lambda b, c: (b, c, idx))

    def full(shape):
        return pl.BlockSpec(shape, lambda b, c: (0,) * len(shape))

    pad_heads = LANE - SSD_HEADS
    args = (
        u, u, u, dt_raw,
        conv_w, conv_b.reshape(1, 2 * w),
        jnp.pad(dt_bias, (0, pad_heads)).reshape(1, LANE), jnp.pad(a_log, (0, pad_heads)).reshape(1, LANE),
        jnp.repeat(d_skip, SSD_HEAD_DIM).reshape(1, w), norm_g.reshape(1, w),
    )
    in_specs = [
        ublock(0), ublock(1), ublock(2),
        pl.BlockSpec((1, CHUNK, LANE), lambda b, c: (b, c, 0)),
        full((SSD_CONV, 2 * w)), full((1, 2 * w)),
        full((1, LANE)), full((1, LANE)), full((1, w)), full((1, w)),
    ]
    return pl.pallas_call(
        _ssd_kernel,
        grid=(bsz, nc),
        in_specs=in_specs,
        out_specs=pl.BlockSpec((1, CHUNK, w), lambda b, c: (b, c, 0)),
        out_shape=jax.ShapeDtypeStruct((bsz, p, w), BF16),
        scratch_shapes=[
            pltpu.VMEM((2 * CHUNK, 2 * w), BF16),
            pltpu.VMEM((SSD_GROUPS, SSD_STATE, GROUP_W), F32),
        ],
        compiler_params=_params(("parallel", "arbitrary"), 32),
        name="ssd_scan",
    )(*args)


def _rotate(x, cos, sin):
    half = RET_DIM // 2
    x1, x2 = x[:, :half], x[:, half:]
    return jnp.concatenate([x1 * cos - x2 * sin, x1 * sin + x2 * cos], axis=-1)


def _ret_kernel(q_ref, k_ref, v_ref, g_ref, cos_ref, sin_ref, dec_ref, zeta_ref, xi_ref, cd_ref, ng_ref,
                o_ref, state):
    c = pl.program_id(1)
    L = CHUNK

    @pl.when(c == 0)
    def _():
        state[...] = jnp.zeros(state.shape, F32)

    valid = _valid_rows(c, L)
    cos = cos_ref[...]
    sin = sin_ref[...]
    for h in range(RET_HEADS):
        sl = slice(h * RET_DIM, (h + 1) * RET_DIM)
        q = jnp.where(valid, _rotate(q_ref[0, :, sl].astype(F32), cos, sin), 0.0)
        k = jnp.where(valid, _rotate(k_ref[0, :, sl].astype(F32), cos, sin) * (RET_DIM ** -0.5), 0.0)
        v16 = jnp.where(valid, v_ref[0, :, sl], jnp.zeros((), BF16))
        q16 = q.astype(BF16)
        scores = _dot_nt(q16, k.astype(BF16)) * dec_ref[h]
        r_prev = state[h]
        o = _dot(scores.astype(BF16), v16) + _dot(q16, r_prev.astype(BF16)) * xi_ref[:, h:h + 1]
        kz = (k * zeta_ref[:, h:h + 1]).T.astype(BF16)
        state[h] = r_prev * cd_ref[:, h:h + 1] + _dot(kz, v16)
        o = o - jnp.mean(o, axis=-1, keepdims=True)
        o = o * lax.rsqrt(jnp.mean(o * o, axis=-1, keepdims=True) + EPS)
        o = o * ng_ref[:, sl]
        o_ref[0, :, sl] = (_silu(g_ref[0, :, sl].astype(F32)) * o).astype(o_ref.dtype)


def _retention(u, norm_g):
    bsz, p, _ = u.shape
    nc = p // CHUNK
    w = RET_HEADS * RET_DIM
    half = RET_DIM // 2
    pos = jnp.arange(p, dtype=F32) - PAD_ROWS
    inv_freq = 1.0 / (10000.0 ** (jnp.arange(half, dtype=F32) / (half - 1)))
    ang = pos[:, None] * inv_freq[None, :]
    log_gamma = jnp.log1p(-jnp.exp2(-5.0 - jnp.arange(RET_HEADS, dtype=F32)))
    idx = jnp.arange(CHUNK, dtype=F32)
    diff = idx[:, None] - idx[None, :]
    decay = jnp.where(diff >= 0, jnp.exp(log_gamma[:, None, None] * jnp.maximum(diff, 0.0)), 0.0)
    pad_heads = LANE - RET_HEADS
    zeta = jnp.pad(jnp.exp(log_gamma[None, :] * (CHUNK - 1 - idx)[:, None]), ((0, 0), (0, pad_heads)))
    xi = jnp.pad(jnp.exp(log_gamma[None, :] * (idx + 1.0)[:, None]), ((0, 0), (0, pad_heads)))
    chunk_decay = jnp.pad(jnp.exp(CHUNK * log_gamma), (0, pad_heads)).reshape(1, LANE)

    def ublock(idx_):
        return pl.BlockSpec((1, CHUNK, w), lambda b, c: (b, c, idx_))

    def full(shape):
        return pl.BlockSpec(shape, lambda b, c: (0,) * len(shape))

    return pl.pallas_call(
        _ret_kernel,
        grid=(bsz, nc),
        in_specs=[
            ublock(3), ublock(4), ublock(5), ublock(6),
            pl.BlockSpec((CHUNK, half), lambda b, c: (c, 0)),
            pl.BlockSpec((CHUNK, half), lambda b, c: (c, 0)),
            full((RET_HEADS, CHUNK, CHUNK)), full((CHUNK, LANE)), full((CHUNK, LANE)), full((1, LANE)),
            full((1, w)),
        ],
        out_specs=pl.BlockSpec((1, CHUNK, w), lambda b, c: (b, c, 0)),
        out_shape=jax.ShapeDtypeStruct((bsz, p, w), BF16),
        scratch_shapes=[pltpu.VMEM((RET_HEADS, RET_DIM, RET_DIM), F32)],
        compiler_params=_params(("parallel", "arbitrary"), 32),
        name="retention",
    )(u, u, u, u, jnp.cos(ang), jnp.sin(ang), decay, zeta, xi, chunk_decay, norm_g.reshape(1, w))


def _mix_ffn_kernel(y1_ref, y2_ref, w1_ref, w2_ref, h_ref, gain_ref, win_ref, cw_ref, cb_ref, wo_ref, o_ref,
                    bufg, bufu):
    t = pl.program_id(1)
    tm = o_ref.shape[1]
    sub = _row_tile(tm, FFN_SUB_ROWS)
    nsub = tm // sub

    @pl.when(t == 0)
    def _():
        bufg[0:HALO, :] = jnp.zeros((HALO, FFN_DIM), F32)
        bufu[0:HALO, :] = jnp.zeros((HALO, FFN_DIM), F32)

    def rows(i):
        return slice(i * sub, (i + 1) * sub)

    def out_proj(i):
        o_ref[0, rows(i), :] = (h_ref[0, rows(i), :] + _dot(y1_ref[0, rows(i), :], w1_ref[...])
                                + _dot(y2_ref[0, rows(i), :], w2_ref[...]))

    def up_proj(i):
        x = o_ref[0, rows(i), :]
        ms = jnp.mean(x * x, axis=-1, keepdims=True)
        xn = ((x * lax.rsqrt(ms + EPS)) * gain_ref[...]).astype(BF16)
        bufg[HALO + i * sub:HALO + (i + 1) * sub, :] = _dot(xn, win_ref[:, :FFN_DIM])
        bufu[HALO + i * sub:HALO + (i + 1) * sub, :] = _dot(xn, win_ref[:, FFN_DIM:])

    def conv(buf, half, i):
        cols = slice(half * FFN_DIM, (half + 1) * FFN_DIM)
        out = cb_ref[:, cols]
        for k in range(FFN_CONV):
            off = HALO - (FFN_CONV - 1) + k + i * sub
            out = out + cw_ref[k:k + 1, cols] * buf[off:off + sub, :]
        return out

    def gate_down(i):
        act = (_silu(conv(bufg, 0, i)) * conv(bufu, 1, i)).astype(BF16)
        out = o_ref[0, rows(i), :] + _dot(act, wo_ref[...])
        r = t * tm + i * sub + lax.broadcasted_iota(jnp.int32, (sub, 1), 0)
        o_ref[0, rows(i), :] = jnp.where(r >= PAD_ROWS, out, 0.0)

    for i in range(nsub):
        out_proj(i)
    for i in range(nsub):
        up_proj(i)
    for i in range(nsub):
        gate_down(i)
    bufg[0:HALO, :] = bufg[tm:tm + HALO, :]
    bufu[0:HALO, :] = bufu[tm:tm + HALO, :]


def _mix_ffn(y1, y2, w1, w2, h, gain, w_in, conv_w, conv_b, w_out):
    bsz, p, d = h.shape
    tm = _row_tile(p, 2 * FFN_SUB_ROWS)
    k1, k2 = y1.shape[-1], y2.shape[-1]

    def tile(width):
        return pl.BlockSpec((1, tm, width), lambda b, t: (b, t, 0))

    return pl.pallas_call(
        _mix_ffn_kernel,
        grid=(bsz, p // tm),
        in_specs=[
            tile(k1), tile(k2), _resident((k1, d)), _resident((k2, d)), tile(d), _resident((1, d)),
            _resident((d, 2 * FFN_DIM)), _resident((FFN_CONV, 2 * FFN_DIM)), _resident((1, 2 * FFN_DIM)),
            _resident((FFN_DIM, d)),
        ],
        out_specs=tile(d),
        out_shape=jax.ShapeDtypeStruct((bsz, p, d), F32),
        scratch_shapes=[pltpu.VMEM((HALO + tm, FFN_DIM), F32), pltpu.VMEM((HALO + tm, FFN_DIM), F32)],
        compiler_params=_params(("parallel", "arbitrary"), 60),
        name="mix_ffn",
    )(y1, y2, w1, w2, h, gain.reshape(1, d), w_in, conv_w, conv_b.reshape(1, 2 * FFN_DIM), w_out)


def _sb_kernel(q_ref, k_ref, v_ref, o_ref, qs_ref, *, nq):
    tq = SB_TILE
    p = q_ref.shape[1]

    def iotas(rows, cols):
        return lax.broadcasted_iota(jnp.int32, (rows, cols), 0), lax.broadcasted_iota(jnp.int32, (rows, cols), 1)

    def stacked_upper(n):
        r, c = iotas(2 * n, n)
        return (jnp.where(r >= n, r - n, r) > c).astype(BF16)

    upper2 = stacked_upper(tq)
    upper2_c = stacked_upper(CHUNK)
    rr, cc = iotas(tq, tq)
    strict = cc < rr
    rr_c, cc_c = iotas(CHUNK, CHUNK)
    meta_mask = jnp.logical_and(cc_c < rr_c, cc_c >= PAD_ROWS)
    meta_key_q = lax.broadcasted_iota(jnp.int32, (tq, CHUNK), 1) >= PAD_ROWS

    def first_head(rows):
        return lax.broadcasted_iota(jnp.int32, (rows, LANE), 1) < SB_HEAD_DIM

    qf = q_ref[0].astype(F32) * (SB_HEAD_DIM ** -0.5)
    qs_ref[0] = jnp.where(first_head(p), qf, 0.0).astype(BF16)
    qs_ref[1] = jnp.where(first_head(p), 0.0, qf).astype(BF16)

    def rows_of(i):
        return slice(CHUNK + i * tq, CHUNK + (i + 1) * tq)

    meta_rows = slice(0, CHUNK)
    tiles = [(meta_rows, meta_rows, meta_mask, upper2_c, "only")]
    for i in range(nq):
        tiles.append((rows_of(i), rows_of(i), strict, upper2, "first"))
        tiles += [(rows_of(i), rows_of(j), None, upper2, "middle") for j in range(i - 1, -1, -1)]
        tiles.append((rows_of(i), meta_rows, meta_key_q, upper2_c, "last"))
    heads = range(2)
    scored, weighted = {}, {}
    acc, car = [None, None], [None, None]

    def scores(t):
        qrows, krows, mask, _, _ = tiles[t]
        k16 = k_ref[0, krows, :]
        zs = [_dot_nt(qs_ref[hd, qrows, :], k16) for hd in heads]
        for hd in heads:
            z = zs[hd]
            sp = jnp.maximum(z, 0.0) + jnp.log(1.0 + jnp.exp2(jnp.abs(z) * (-LOG2E)))
            if mask is not None:
                sp = jnp.where(mask, sp, 0.0)
            hi = sp.astype(BF16)
            lo = (sp - hi.astype(F32)).astype(BF16)
            scored[t, hd] = (jnp.concatenate([hi, lo], axis=1), z - sp, sp[:, 0:1])

    def weights(t):
        _, _, mask, upper, _ = tiles[t]
        newer = [_dot(scored[t, hd][0], upper) for hd in heads]
        for hd in heads:
            _, logsig, sp0 = scored.pop((t, hd))
            w = jnp.exp(logsig - newer[hd])
            if mask is not None:
                w = jnp.where(mask, w, 0.0)
            weighted[t, hd] = (w.astype(BF16), jnp.broadcast_to(newer[hd][:, 0:1] + sp0, (w.shape[0], LANE)))

    def products(t):
        qrows, krows, _, _, role = tiles[t]
        v16 = v_ref[0, krows, :]
        pvs = [_dot(weighted[t, hd][0], v16) for hd in heads]
        for hd in heads:
            _, rowsum = weighted.pop((t, hd))
            if role in ("only", "first"):
                acc[hd], car[hd] = pvs[hd], rowsum
            else:
                acc[hd] = acc[hd] + pvs[hd] * jnp.exp(-car[hd])
                car[hd] = car[hd] + rowsum
        if role in ("only", "last"):
            o_ref[0, qrows, :] = jnp.where(first_head(acc[0].shape[0]), acc[0], acc[1]).astype(o_ref.dtype)

    for t in range(len(tiles) + 2):
        if t < len(tiles):
            scores(t)
        if 1 <= t <= len(tiles):
            weights(t - 1)
        if t >= 2:
            products(t - 2)


def _sb_attention(u):
    bsz, p, _ = u.shape
    pairs = SB_WIDTH // LANE
    nq = (p - CHUNK) // SB_TILE
    assert nq * SB_TILE + CHUNK == p

    def ublock(base):
        return pl.BlockSpec((1, p, LANE), lambda b, hp: (b, 0, base + hp))

    return pl.pallas_call(
        functools.partial(_sb_kernel, nq=nq),
        grid=(bsz, pairs),
        in_specs=[ublock(0), ublock(pairs), ublock(2 * pairs)],
        out_specs=pl.BlockSpec((1, p, LANE), lambda b, hp: (b, 0, hp)),
        out_shape=jax.ShapeDtypeStruct((bsz, p, SB_WIDTH), BF16),
        scratch_shapes=[pltpu.VMEM((2, p, LANE), BF16)],
        compiler_params=_params(("parallel", "parallel"), 48),
        name="stickbreak_attention",
    )(u, u, u)


def _gelu_tanh(x):
    return 0.5 * x * (1.0 + jnp.tanh(math.sqrt(2.0 / math.pi) * (x + 0.044715 * (x * x * x))))


def _lru_kernel(gate_ref, xr_ref, cw_ref, cb_ref, wax_ref, ba_ref, bx_ref, lam_ref, o_ref, hist, hcar):
    t = pl.program_id(1)
    L = o_ref.shape[1]
    w = LRU_WIDTH

    @pl.when(t == 0)
    def _():
        hist[0:L, :] = jnp.zeros((L, w), BF16)
        hcar[...] = jnp.zeros(hcar.shape, F32)

    x = _causal_conv(hist, xr_ref[0], cw_ref, cb_ref, LRU_CONV)
    x16 = x.astype(BF16)
    gates = [_dot(x16[:, n * LRU_BLOCK:(n + 1) * LRU_BLOCK], wax_ref[n]) for n in range(LRU_BLOCKS)]
    r = _sigmoid(jnp.concatenate([g[:, :LRU_BLOCK] for g in gates], axis=-1) + ba_ref[...])
    i = _sigmoid(jnp.concatenate([g[:, LRU_BLOCK:] for g in gates], axis=-1) + bx_ref[...])
    log_a = -LRU_C * r * _softplus(-lam_ref[...])
    a = jnp.exp(log_a)
    gap = 1.0 - jnp.exp(2.0 * log_a)
    b = jnp.where(gap > 0.0, gap * lax.rsqrt(gap), 0.0) * (i * x)
    valid = (t * L + lax.broadcasted_iota(jnp.int32, (L, 1), 0)) >= PAD_ROWS
    b = jnp.where(valid, b, 0.0)

    groups = L // HALO
    a = a.reshape(groups, HALO, w)
    b = b.reshape(groups, HALO, w)
    sub = lax.broadcasted_iota(jnp.int32, (1, HALO, 1), 1)
    k = 1
    while k < HALO:
        keep = sub >= k
        a_sh = jnp.where(keep, pltpu.roll(a, k, 1), 1.0)
        b_sh = jnp.where(keep, pltpu.roll(b, k, 1), 0.0)
        b = a * b_sh + b
        a = a * a_sh
        k *= 2
    carry = hcar[...]
    rows = []
    for g in range(groups):
        hg = a[g] * carry + b[g]
        rows.append(hg)
        carry = hg[HALO - 1:HALO, :]
    hcar[...] = carry
    hs = jnp.concatenate(rows, axis=0)
    o_ref[0] = (hs * _gelu_tanh(gate_ref[0].astype(F32))).astype(o_ref.dtype)


def _rg_lru(u, conv_w, conv_b, w_ax, b_a, b_x, lam):
    bsz, p, _ = u.shape
    w = LRU_WIDTH
    tt = CHUNK

    def full(shape):
        return pl.BlockSpec(shape, lambda b, t: (0,) * len(shape))

    return pl.pallas_call(
        _lru_kernel,
        grid=(bsz, p // tt),
        in_specs=[
            pl.BlockSpec((1, tt, w), lambda b, t: (b, t, 3)),
            pl.BlockSpec((1, tt, w), lambda b, t: (b, t, 4)),
            full((LRU_CONV, w)), full((1, w)), full((LRU_BLOCKS, LRU_BLOCK, 2 * LRU_BLOCK)),
            full((1, w)), full((1, w)), full((1, w)),
        ],
        out_specs=pl.BlockSpec((1, tt, w), lambda b, t: (b, t, 0)),
        out_shape=jax.ShapeDtypeStruct((bsz, p, w), BF16),
        scratch_shapes=[pltpu.VMEM((2 * tt, w), BF16), pltpu.VMEM((1, w), F32)],
        compiler_params=_params(("parallel", "arbitrary"), 32),
        name="rg_lru",
    )(u, u, conv_w, conv_b.reshape(1, w), w_ax, b_a.reshape(1, w), b_x.reshape(1, w), lam.reshape(1, w))


def _final_norm_kernel(h_ref, g_ref, o_ref):
    x = h_ref[0, CHUNK:, :]
    ms = jnp.mean(x * x, axis=-1, keepdims=True)
    o_ref[0] = (x * lax.rsqrt(ms + EPS)) * g_ref[...]


def _final_norm(h, gain):
    bsz, p, d = h.shape
    return pl.pallas_call(
        _final_norm_kernel,
        grid=(bsz,),
        in_specs=[pl.BlockSpec((1, p, d), lambda b: (b, 0, 0)), pl.BlockSpec((1, d), lambda b: (0, 0))],
        out_specs=pl.BlockSpec((1, p - CHUNK, d), lambda b: (b, 0, 0)),
        out_shape=jax.ShapeDtypeStruct((bsz, p - CHUNK, d), F32),
        compiler_params=_params(("parallel",), 48),
        name="final_norm",
    )(h, gain.reshape(1, d))


def kernel(x, meta_tokens, l0_mix_norm, l0_w_in, l0_ssd_conv_w, l0_ssd_conv_b, l0_ssd_dt_bias, l0_ssd_a_log,
           l0_ssd_d, l0_ssd_norm, l0_ret_norm, l0_w_out, l0_ffn_norm, l0_ffn_w_in, l0_ffn_conv_w, l0_ffn_conv_b,
           l0_ffn_w_out, l1_mix_norm, l1_w_in, l1_lru_conv_w, l1_lru_conv_b, l1_lru_wa, l1_lru_ba, l1_lru_wx,
           l1_lru_bx, l1_lru_lambda, l1_w_out, l1_ffn_norm, l1_ffn_w_in, l1_ffn_conv_w, l1_ffn_conv_b,
           l1_ffn_w_out, final_norm):
    bsz, seq, d = x.shape
    p = seq + CHUNK
    assert d == D_MODEL and seq % SB_TILE == 0

    lead = jnp.concatenate([jnp.zeros((PAD_ROWS, d), x.dtype), meta_tokens.astype(x.dtype)], 0)
    h = jnp.concatenate([jnp.broadcast_to(lead[None], (bsz, CHUNK, d)), x], 1)

    dt0 = SSD_INNER + SSD_INNER + 2 * SSD_GROUPS * SSD_STATE
    w_main = jnp.concatenate([l0_w_in[:, :dt0], l0_w_in[:, dt0 + SSD_HEADS:]], 1).astype(BF16)
    w_dt = jnp.pad(l0_w_in[:, dt0:dt0 + SSD_HEADS], ((0, 0), (0, LANE - SSD_HEADS))).astype(BF16)
    u, dt_raw = _norm_matmul(h.reshape(bsz * p, d), l0_mix_norm, w_main, w_dt)
    u = u.reshape(bsz, p, -1)
    y_ssd = _ssd(u, dt_raw.reshape(bsz, p, LANE), l0_ssd_conv_w, l0_ssd_conv_b, l0_ssd_dt_bias, l0_ssd_a_log,
                 l0_ssd_d, l0_ssd_norm)
    y_ret = _retention(u, l0_ret_norm)
    w_out = l0_w_out.astype(BF16)
    h = _mix_ffn(y_ssd, y_ret, w_out[:SSD_INNER], w_out[SSD_INNER:], h, l0_ffn_norm, l0_ffn_w_in.astype(BF16),
                 l0_ffn_conv_w, l0_ffn_conv_b, l0_ffn_w_out.astype(BF16))

    u = _norm_matmul(h.reshape(bsz * p, d), l1_mix_norm, l1_w_in.astype(BF16)).reshape(bsz, p, -1)
    y_sb = _sb_attention(u)
    w_ax = jnp.concatenate([l1_lru_wa, l1_lru_wx], -1).astype(BF16)
    y_lru = _rg_lru(u, l1_lru_conv_w, l1_lru_conv_b, w_ax, l1_lru_ba, l1_lru_bx, l1_lru_lambda)
    w_out = l1_w_out.astype(BF16)
    h = _mix_ffn(y_sb, y_lru, w_out[:SB_WIDTH], w_out[SB_WIDTH:], h, l1_ffn_norm, l1_ffn_w_in.astype(BF16),
                 l1_ffn_conv_w, l1_ffn_conv_b, l1_ffn_w_out.astype(BF16))

    return _final_norm(h, final_norm)
```

```python
import functools
import math

import jax
import jax.numpy as jnp
from jax import lax
from jax.experimental import pallas as pl
from jax.experimental.pallas import tpu as pltpu

F32 = jnp.float32
BF16 = jnp.bfloat16

D_MODEL = 1024
N_META = 16
CHUNK = 128
PAD_ROWS = CHUNK - N_META
EPS = 1e-6

SSD_HEADS = 16
SSD_HEAD_DIM = 64
SSD_INNER = SSD_HEADS * SSD_HEAD_DIM
SSD_GROUPS = 4
SSD_STATE = 128
SSD_CONV = 4
GROUP_W = SSD_INNER // SSD_GROUPS
HEADS_PER_GROUP = SSD_HEADS // SSD_GROUPS

RET_HEADS = 4
RET_DIM = 256

SB_HEADS = 16
SB_HEAD_DIM = 64
SB_WIDTH = SB_HEADS * SB_HEAD_DIM
SB_TILE = 256

LRU_WIDTH = 1024
LRU_BLOCKS = 8
LRU_BLOCK = LRU_WIDTH // LRU_BLOCKS
LRU_CONV = 4
LRU_C = 8.0

FFN_DIM = 2816
FFN_CONV = 3
FFN_SUB_ROWS = 272

LANE = 128
HALO = 8
MIB = 1024 * 1024
LOG2E = 1.4426950408889634


def _params(sem, vmem_mib):
    return pltpu.CompilerParams(dimension_semantics=sem, vmem_limit_bytes=vmem_mib * MIB)


def _row_tile(n, cap, mult=16):
    best = None
    for t in range(mult, min(n, cap) + 1, mult):
        if n % t == 0:
            best = t
    assert best is not None, (n, cap)
    return best


def _sigmoid(x):
    return 1.0 / (1.0 + jnp.exp(-x))


def _silu(x):
    return x * _sigmoid(x)


def _softplus(x):
    return jnp.maximum(x, 0.0) + jnp.log1p(jnp.exp(-jnp.abs(x)))


def _dot(a, b):
    return jnp.dot(a, b, preferred_element_type=F32)


def _dot_nt(a, b):
    return lax.dot_general(a, b, (((1,), (1,)), ((), ())), preferred_element_type=F32)


def _split3_dot(tri_bf16, x):
    x1 = x.astype(BF16)
    r1 = x - x1.astype(F32)
    x2 = r1.astype(BF16)
    x3 = (r1 - x2.astype(F32)).astype(BF16)
    return _dot(tri_bf16, x1) + _dot(tri_bf16, x2) + _dot(tri_bf16, x3)


def _norm_matmul_kernel(x_ref, g_ref, w_ref, *rest, has_aux, tn):
    if has_aux:
        waux_ref, o_ref, aux_ref = rest
    else:
        (o_ref,) = rest
    x = x_ref[...]
    ms = jnp.mean(x * x, axis=-1, keepdims=True)
    xn = ((x * lax.rsqrt(ms + EPS)) * g_ref[...]).astype(BF16)
    if has_aux:
        aux_ref[...] = _dot(xn, waux_ref[...])
    for j in range(o_ref.shape[1] // tn):
        o_ref[:, j * tn:(j + 1) * tn] = _dot(xn, w_ref[:, j * tn:(j + 1) * tn]).astype(o_ref.dtype)


def _resident(shape):
    return pl.BlockSpec(shape, lambda *_: (0,) * len(shape), pipeline_mode=pl.Buffered(1))


def _norm_matmul(h2d, gain, w, w_aux=None):
    m, d = h2d.shape
    n = w.shape[1]
    tm = _row_tile(m, 512)
    tn = 1024
    assert n % tn == 0
    has_aux = w_aux is not None
    in_specs = [pl.BlockSpec((tm, d), lambda i: (i, 0)), _resident((1, d)), _resident((d, n))]
    out_shape = [jax.ShapeDtypeStruct((m, n), BF16)]
    out_specs = [pl.BlockSpec((tm, n), lambda i: (i, 0))]
    args = [h2d, gain.reshape(1, d), w]
    if has_aux:
        in_specs.append(_resident((d, LANE)))
        out_shape.append(jax.ShapeDtypeStruct((m, LANE), F32))
        out_specs.append(pl.BlockSpec((tm, LANE), lambda i: (i, 0)))
        args.append(w_aux)
    outs = pl.pallas_call(
        functools.partial(_norm_matmul_kernel, has_aux=has_aux, tn=tn),
        grid=(m // tm,),
        in_specs=in_specs,
        out_specs=out_specs,
        out_shape=out_shape,
        compiler_params=_params(("parallel",), 48),
        name="norm_matmul_aux" if has_aux else "norm_matmul",
    )(*args)
    return outs if has_aux else outs[0]


def _valid_rows(chunk_idx, rows):
    r = lax.broadcasted_iota(jnp.int32, (rows, 1), 0)
    return jnp.logical_or(chunk_idx > 0, r >= PAD_ROWS)


def _causal_conv(hist_ref, cur16, w_ref, b_ref, taps):
    rows = cur16.shape[0]
    hist_ref[rows:2 * rows, :] = cur16
    both = hist_ref[...]
    t = lax.broadcasted_iota(jnp.int32, (rows, 2 * rows), 0)
    j = lax.broadcasted_iota(jnp.int32, (rows, 2 * rows), 1)
    out = b_ref[...] + w_ref[taps - 1:taps, :] * cur16.astype(F32)
    for k in range(taps - 1):
        back = taps - 1 - k
        shift = (j == t + (rows - back)).astype(BF16)
        out = out + w_ref[k:k + 1, :] * _dot(shift, both)
    hist_ref[0:rows, :] = cur16
    return out


def _rep_heads(cols, g):
    rows = cols.shape[0]
    lane = lax.broadcasted_iota(jnp.int32, (rows, GROUP_W), 1)
    h0 = g * HEADS_PER_GROUP
    out = jnp.broadcast_to(cols[:, h0 + 3:h0 + 4], (rows, GROUP_W))
    for hh in (2, 1, 0):
        out = jnp.where(lane < (hh + 1) * SSD_HEAD_DIM, cols[:, h0 + hh:h0 + hh + 1], out)
    return out


def _ssd_kernel(z_ref, xs_ref, bc_ref, dt_ref, cw_ref, cb_ref, dtb_ref, alog_ref, dskip_ref, ng_ref, o_ref,
                hist, state):
    c = pl.program_id(1)
    L = CHUNK

    @pl.when(c == 0)
    def _():
        hist[0:L, :] = jnp.zeros((L, hist.shape[1]), BF16)
        state[...] = jnp.zeros(state.shape, F32)

    valid = _valid_rows(c, L)
    xbc = _causal_conv(hist, jnp.concatenate([xs_ref[0], bc_ref[0]], axis=-1), cw_ref, cb_ref, SSD_CONV)
    xbc = jnp.where(valid, _silu(xbc), 0.0)
    xs = xbc[:, :SSD_INNER]
    bc = xbc[:, SSD_INNER:]

    dt = jnp.where(valid, _softplus(dt_ref[0] + dtb_ref[...]), 0.0)
    a_dt = dt * (-jnp.exp(alog_ref[...]))
    row = lax.broadcasted_iota(jnp.int32, (L, L), 0)
    col = lax.broadcasted_iota(jnp.int32, (L, L), 1)
    lower = row >= col
    a_col = _split3_dot(lower.astype(BF16), a_dt)
    a_row = a_col.T
    lane = lax.broadcasted_iota(jnp.int32, (L, GROUP_W), 1)
    groups = range(SSD_GROUPS)

    bm = [bc[:, g * SSD_STATE:(g + 1) * SSD_STATE] for g in groups]
    cm16 = [bc[:, (SSD_GROUPS + g) * SSD_STATE:(SSD_GROUPS + g + 1) * SSD_STATE].astype(BF16) for g in groups]
    s_prev = [state[g] for g in groups]
    cb = [_dot_nt(cm16[g], bm[g].astype(BF16)) for g in groups]
    y_off = [_dot(cm16[g], s_prev[g].astype(BF16)) for g in groups]
    a_rep = [_rep_heads(a_col, g) for g in groups]
    xg = [xs[:, g * GROUP_W:(g + 1) * GROUP_W] for g in groups]
    xdt = [xg[g] * _rep_heads(dt, g) for g in groups]
    xdt16 = [x.astype(BF16) for x in xdt]
    scores = []
    for h in range(SSD_HEADS):
        seg = a_col[:, h:h + 1] - a_row[h:h + 1, :]
        decay = jnp.where(lower, jnp.exp(seg), 0.0)
        scores.append((cb[h // HEADS_PER_GROUP] * decay).astype(BF16))
    y_diag = [_dot(scores[h], xdt16[h // HEADS_PER_GROUP]) for h in range(SSD_HEADS)]
    a_last = [a[L - 1:L, :] for a in a_rep]
    xdec = [(xdt[g] * jnp.exp(a_last[g] - a_rep[g])).astype(BF16) for g in groups]
    s_new = [_dot(bm[g].T.astype(BF16), xdec[g]) for g in groups]
    for g in groups:
        sl = slice(g * GROUP_W, (g + 1) * GROUP_W)
        state[g] = s_prev[g] * jnp.exp(a_last[g]) + s_new[g]
        y = y_off[g] * jnp.exp(a_rep[g])
        for hh in range(HEADS_PER_GROUP):
            own = jnp.logical_and(lane >= hh * SSD_HEAD_DIM, lane < (hh + 1) * SSD_HEAD_DIM)
            y = y + jnp.where(own, y_diag[g * HEADS_PER_GROUP + hh], 0.0)
        y = y + xg[g] * dskip_ref[:, sl]
        y = y * _silu(z_ref[0, :, sl].astype(F32))
        y = y * lax.rsqrt(jnp.mean(y * y, axis=-1, keepdims=True) + EPS)
        o_ref[0, :, sl] = (y * ng_ref[:, sl]).astype(o_ref.dtype)


def _ssd(u, dt_raw, conv_w, conv_b, dt_bias, a_log, d_skip, norm_g):
    bsz, p, _ = u.shape
    nc = p // CHUNK
    w = SSD_INNER

    def ublock(idx):
        return pl.BlockSpec((1, CHUNK, w), lambda b, c: (b, c, idx))

    def full(shape):
        return pl.BlockSpec(shape, lambda b, c: (0,) * len(shape))

    pad_heads = LANE - SSD_HEADS
    args = (
        u, u, u, dt_raw,
        conv_w, conv_b.reshape(1, 2 * w),
        jnp.pad(dt_bias, (0, pad_heads)).reshape(1, LANE), jnp.pad(a_log, (0, pad_heads)).reshape(1, LANE),
        jnp.repeat(d_skip, SSD_HEAD_DIM).reshape(1, w), norm_g.reshape(1, w),
    )
    in_specs = [
        ublock(0), ublock(1), ublock(2),
        pl.BlockSpec((1, CHUNK, LANE), lambda b, c: (b, c, 0)),
        full((SSD_CONV, 2 * w)), full((1, 2 * w)),
        full((1, LANE)), full((1, LANE)), full((1, w)), full((1, w)),
    ]
    return pl.pallas_call(
        _ssd_kernel,
        grid=(bsz, nc),
        in_specs=in_specs,
        out_specs=pl.BlockSpec((1, CHUNK, w), lambda b, c: (b, c, 0)),
        out_shape=jax.ShapeDtypeStruct((bsz, p, w), BF16),
        scratch_shapes=[
            pltpu.VMEM((2 * CHUNK, 2 * w), BF16),
            pltpu.VMEM((SSD_GROUPS, SSD_STATE, GROUP_W), F32),
        ],
        compiler_params=_params(("parallel", "arbitrary"), 32),
        name="ssd_scan",
    )(*args)


def _rotate(x, cos, sin):
    half = RET_DIM // 2
    x1, x2 = x[:, :half], x[:, half:]
    return jnp.concatenate([x1 * cos - x2 * sin, x1 * sin + x2 * cos], axis=-1)


def _ret_kernel(q_ref, k_ref, v_ref, g_ref, cos_ref, sin_ref, dec_ref, zeta_ref, xi_ref, cd_ref, ng_ref,
                o_ref, state):
    c = pl.program_id(1)
    L = CHUNK

    @pl.when(c == 0)
    def _():
        state[...] = jnp.zeros(state.shape, F32)

    valid = _valid_rows(c, L)
    cos = cos_ref[...]
    sin = sin_ref[...]
    heads = range(RET_HEADS)
    cols = [slice(h * RET_DIM, (h + 1) * RET_DIM) for h in heads]

    q16 = [jnp.where(valid, _rotate(q_ref[0, :, cols[h]].astype(F32), cos, sin), 0.0).astype(BF16) for h in heads]
    k = [jnp.where(valid, _rotate(k_ref[0, :, cols[h]].astype(F32), cos, sin) * (RET_DIM ** -0.5), 0.0)
         for h in heads]
    v16 = [jnp.where(valid, v_ref[0, :, cols[h]], jnp.zeros((), BF16)) for h in heads]
    r_prev = [state[h] for h in heads]
    qk = [_dot_nt(q16[h], k[h].astype(BF16)) for h in heads]
    cross = [_dot(q16[h], r_prev[h].astype(BF16)) for h in heads]
    scores = [(qk[h] * dec_ref[h]).astype(BF16) for h in heads]
    kz = [(k[h] * zeta_ref[:, h:h + 1]).T.astype(BF16) for h in heads]
    inner = [_dot(scores[h], v16[h]) for h in heads]
    kv = [_dot(kz[h], v16[h]) for h in heads]
    for h in heads:
        state[h] = r_prev[h] * cd_ref[:, h:h + 1] + kv[h]
        o = inner[h] + cross[h] * xi_ref[:, h:h + 1]
        o = o - jnp.mean(o, axis=-1, keepdims=True)
        o = o * lax.rsqrt(jnp.mean(o * o, axis=-1, keepdims=True) + EPS)
        o = o * ng_ref[:, cols[h]]
        o_ref[0, :, cols[h]] = (_silu(g_ref[0, :, cols[h]].astype(F32)) * o).astype(o_ref.dtype)


def _retention(u, norm_g):
    bsz, p, _ = u.shape
    nc = p // CHUNK
    w = RET_HEADS * RET_DIM
    half = RET_DIM // 2
    pos = jnp.arange(p, dtype=F32) - PAD_ROWS
    inv_freq = 1.0 / (10000.0 ** (jnp.arange(half, dtype=F32) / (half - 1)))
    ang = pos[:, None] * inv_freq[None, :]
    log_gamma = jnp.log1p(-jnp.exp2(-5.0 - jnp.arange(RET_HEADS, dtype=F32)))
    idx = jnp.arange(CHUNK, dtype=F32)
    diff = idx[:, None] - idx[None, :]
    decay = jnp.where(diff >= 0, jnp.exp(log_gamma[:, None, None] * jnp.maximum(diff, 0.0)), 0.0)
    pad_heads = LANE - RET_HEADS
    zeta = jnp.pad(jnp.exp(log_gamma[None, :] * (CHUNK - 1 - idx)[:, None]), ((0, 0), (0, pad_heads)))
    xi = jnp.pad(jnp.exp(log_gamma[None, :] * (idx + 1.0)[:, None]), ((0, 0), (0, pad_heads)))
    chunk_decay = jnp.pad(jnp.exp(CHUNK * log_gamma), (0, pad_heads)).reshape(1, LANE)

    def ublock(idx_):
        return pl.BlockSpec((1, CHUNK, w), lambda b, c: (b, c, idx_))

    def full(shape):
        return pl.BlockSpec(shape, lambda b, c: (0,) * len(shape))

    return pl.pallas_call(
        _ret_kernel,
        grid=(bsz, nc),
        in_specs=[
            ublock(3), ublock(4), ublock(5), ublock(6),
            pl.BlockSpec((CHUNK, half), lambda b, c: (c, 0)),
            pl.BlockSpec((CHUNK, half), lambda b, c: (c, 0)),
            full((RET_HEADS, CHUNK, CHUNK)), full((CHUNK, LANE)), full((CHUNK, LANE)), full((1, LANE)),
            full((1, w)),
        ],
        out_specs=pl.BlockSpec((1, CHUNK, w), lambda b, c: (b, c, 0)),
        out_shape=jax.ShapeDtypeStruct((bsz, p, w), BF16),
        scratch_shapes=[pltpu.VMEM((RET_HEADS, RET_DIM, RET_DIM), F32)],
        compiler_params=_params(("parallel", "arbitrary"), 32),
        name="retention",
    )(u, u, u, u, jnp.cos(ang), jnp.sin(ang), decay, zeta, xi, chunk_decay, norm_g.reshape(1, w))


def _mix_ffn_kernel(y1_ref, y2_ref, w1_ref, w2_ref, h_ref, gain_ref, win_ref, cw_ref, cb_ref, wo_ref, o_ref,
                    bufg, bufu):
    t = pl.program_id(1)
    tm = o_ref.shape[1]
    sub = _row_tile(tm, FFN_SUB_ROWS)
    nsub = tm // sub

    @pl.when(t == 0)
    def _():
        bufg[0:HALO, :] = jnp.zeros((HALO, FFN_DIM), F32)
        bufu[0:HALO, :] = jnp.zeros((HALO, FFN_DIM), F32)

    def rows(i):
        return slice(i * sub, (i + 1) * sub)

    def out_proj(i):
        o_ref[0, rows(i), :] = (h_ref[0, rows(i), :] + _dot(y1_ref[0, rows(i), :], w1_ref[...])
                                + _dot(y2_ref[0, rows(i), :], w2_ref[...]))

    def up_proj(i):
        x = o_ref[0, rows(i), :]
        ms = jnp.mean(x * x, axis=-1, keepdims=True)
        xn = ((x * lax.rsqrt(ms + EPS)) * gain_ref[...]).astype(BF16)
        bufg[HALO + i * sub:HALO + (i + 1) * sub, :] = _dot(xn, win_ref[:, :FFN_DIM])
        bufu[HALO + i * sub:HALO + (i + 1) * sub, :] = _dot(xn, win_ref[:, FFN_DIM:])

    def conv(buf, half, i):
        cols = slice(half * FFN_DIM, (half + 1) * FFN_DIM)
        out = cb_ref[:, cols]
        for k in range(FFN_CONV):
            off = HALO - (FFN_CONV - 1) + k + i * sub
            out = out + cw_ref[k:k + 1, cols] * buf[off:off + sub, :]
        return out

    def gate_down(i):
        act = (_silu(conv(bufg, 0, i)) * conv(bufu, 1, i)).astype(BF16)
        out = o_ref[0, rows(i), :] + _dot(act, wo_ref[...])
        r = t * tm + i * sub + lax.broadcasted_iota(jnp.int32, (sub, 1), 0)
        o_ref[0, rows(i), :] = jnp.where(r >= PAD_ROWS, out, 0.0)

    for i in range(nsub):
        out_proj(i)
    for i in range(nsub):
        up_proj(i)
    for i in range(nsub):
        gate_down(i)
    bufg[0:HALO, :] = bufg[tm:tm + HALO, :]
    bufu[0:HALO, :] = bufu[tm:tm + HALO, :]


def _mix_ffn(y1, y2, w1, w2, h, gain, w_in, conv_w, conv_b, w_out):
    bsz, p, d = h.shape
    tm = _row_tile(p, 2 * FFN_SUB_ROWS)
    k1, k2 = y1.shape[-1], y2.shape[-1]

    def tile(width):
        return pl.BlockSpec((1, tm, width), lambda b, t: (b, t, 0))

    return pl.pallas_call(
        _mix_ffn_kernel,
        grid=(bsz, p // tm),
        in_specs=[
            tile(k1), tile(k2), _resident((k1, d)), _resident((k2, d)), tile(d), _resident((1, d)),
            _resident((d, 2 * FFN_DIM)), _resident((FFN_CONV, 2 * FFN_DIM)), _resident((1, 2 * FFN_DIM)),
            _resident((FFN_DIM, d)),
        ],
        out_specs=tile(d),
        out_shape=jax.ShapeDtypeStruct((bsz, p, d), F32),
        scratch_shapes=[pltpu.VMEM((HALO + tm, FFN_DIM), F32), pltpu.VMEM((HALO + tm, FFN_DIM), F32)],
        compiler_params=_params(("parallel", "arbitrary"), 60),
        name="mix_ffn",
    )(y1, y2, w1, w2, h, gain.reshape(1, d), w_in, conv_w, conv_b.reshape(1, 2 * FFN_DIM), w_out)


def _sb_kernel(q_ref, k_ref, v_ref, o_ref, qs_ref, *, nq):
    tq = SB_TILE
    p = q_ref.shape[1]

    def iotas(rows, cols):
        return lax.broadcasted_iota(jnp.int32, (rows, cols), 0), lax.broadcasted_iota(jnp.int32, (rows, cols), 1)

    def newer_keys(n):
        r, c = iotas(n, n)
        return (r > c).astype(BF16)

    upper2 = newer_keys(tq)
    upper2_c = newer_keys(CHUNK)
    rr, cc = iotas(tq, tq)
    strict = cc < rr
    rr_c, cc_c = iotas(CHUNK, CHUNK)
    meta_mask = jnp.logical_and(cc_c < rr_c, cc_c >= PAD_ROWS)
    meta_key_q = lax.broadcasted_iota(jnp.int32, (tq, CHUNK), 1) >= PAD_ROWS

    def first_head(rows):
        return lax.broadcasted_iota(jnp.int32, (rows, LANE), 1) < SB_HEAD_DIM

    qf = q_ref[0].astype(F32) * (SB_HEAD_DIM ** -0.5)
    qs_ref[0] = jnp.where(first_head(p), qf, 0.0).astype(BF16)
    qs_ref[1] = jnp.where(first_head(p), 0.0, qf).astype(BF16)

    def rows_of(i):
        return slice(CHUNK + i * tq, CHUNK + (i + 1) * tq)

    meta_rows = slice(0, CHUNK)
    tiles = [(meta_rows, meta_rows, meta_mask, upper2_c, "only")]
    for i in range(nq):
        tiles.append((rows_of(i), rows_of(i), strict, upper2, "first"))
        tiles += [(rows_of(i), rows_of(j), None, upper2, "middle") for j in range(i - 1, -1, -1)]
        tiles.append((rows_of(i), meta_rows, meta_key_q, upper2_c, "last"))
    heads = range(2)
    scored, weighted = {}, {}
    acc, car = [None, None], [None, None]

    def scores(t):
        qrows, krows, mask, _, _ = tiles[t]
        k16 = k_ref[0, krows, :]
        zs = [_dot_nt(qs_ref[hd, qrows, :], k16) for hd in heads]
        for hd in heads:
            z = zs[hd]
            sp = jnp.maximum(z, 0.0) + jnp.log(1.0 + jnp.exp2(jnp.abs(z) * (-LOG2E)))
            if mask is not None:
                sp = jnp.where(mask, sp, 0.0)
            scored[t, hd] = (sp.astype(BF16), z - sp, sp[:, 0:1])

    def weights(t):
        _, _, mask, upper, _ = tiles[t]
        newer = [_dot(scored[t, hd][0], upper) for hd in heads]
        for hd in heads:
            _, logsig, sp0 = scored.pop((t, hd))
            w = jnp.exp(logsig - newer[hd])
            if mask is not None:
                w = jnp.where(mask, w, 0.0)
            weighted[t, hd] = (w.astype(BF16), jnp.broadcast_to(newer[hd][:, 0:1] + sp0, (w.shape[0], LANE)))

    def products(t):
        qrows, krows, _, _, role = tiles[t]
        v16 = v_ref[0, krows, :]
        pvs = [_dot(weighted[t, hd][0], v16) for hd in heads]
        for hd in heads:
            _, rowsum = weighted.pop((t, hd))
            if role in ("only", "first"):
                acc[hd], car[hd] = pvs[hd], rowsum
            else:
                acc[hd] = acc[hd] + pvs[hd] * jnp.exp(-car[hd])
                car[hd] = car[hd] + rowsum
        if role in ("only", "last"):
            o_ref[0, qrows, :] = jnp.where(first_head(acc[0].shape[0]), acc[0], acc[1]).astype(o_ref.dtype)

    for t in range(len(tiles) + 2):
        if t < len(tiles):
            scores(t)
        if 1 <= t <= len(tiles):
            weights(t - 1)
        if t >= 2:
            products(t - 2)


def _sb_attention(u):
    bsz, p, _ = u.shape
    pairs = SB_WIDTH // LANE
    nq = (p - CHUNK) // SB_TILE
    assert nq * SB_TILE + CHUNK == p

    def ublock(base):
        return pl.BlockSpec((1, p, LANE), lambda b, hp: (b, 0, base + hp))

    return pl.pallas_call(
        functools.partial(_sb_kernel, nq=nq),
        grid=(bsz, pairs),
        in_specs=[ublock(0), ublock(pairs), ublock(2 * pairs)],
        out_specs=pl.BlockSpec((1, p, LANE), lambda b, hp: (b, 0, hp)),
        out_shape=jax.ShapeDtypeStruct((bsz, p, SB_WIDTH), BF16),
        scratch_shapes=[pltpu.VMEM((2, p, LANE), BF16)],
        compiler_params=_params(("parallel", "parallel"), 48),
        name="stickbreak_attention",
    )(u, u, u)


def _gelu_tanh(x):
    return 0.5 * x * (1.0 + jnp.tanh(math.sqrt(2.0 / math.pi) * (x + 0.044715 * (x * x * x))))


def _lru_kernel(gate_ref, xr_ref, cw_ref, cb_ref, wax_ref, ba_ref, bx_ref, lam_ref, o_ref, hist):
    L = CHUNK
    w = LRU_WIDTH
    tiles = o_ref.shape[1] // L
    hist[0:L, :] = jnp.zeros((L, w), BF16)
    soft_lam = _softplus(-lam_ref[...])
    sub = lax.broadcasted_iota(jnp.int32, (1, HALO, 1), 1)
    groups = L // HALO
    carry = jnp.zeros((1, w), F32)
    for t in range(tiles):
        rows = slice(t * L, (t + 1) * L)
        x = _causal_conv(hist, xr_ref[0, rows, :], cw_ref, cb_ref, LRU_CONV)
        x16 = x.astype(BF16)
        gates = [_dot(x16[:, n * LRU_BLOCK:(n + 1) * LRU_BLOCK], wax_ref[n]) for n in range(LRU_BLOCKS)]
        r = _sigmoid(jnp.concatenate([g[:, :LRU_BLOCK] for g in gates], axis=-1) + ba_ref[...])
        i = _sigmoid(jnp.concatenate([g[:, LRU_BLOCK:] for g in gates], axis=-1) + bx_ref[...])
        log_a = -LRU_C * r * soft_lam
        a = jnp.exp(log_a)
        gap = 1.0 - jnp.exp(2.0 * log_a)
        b = jnp.where(gap > 0.0, gap * lax.rsqrt(gap), 0.0) * (i * x)
        if t == 0:
            b = jnp.where(lax.broadcasted_iota(jnp.int32, (L, 1), 0) >= PAD_ROWS, b, 0.0)

        a = a.reshape(groups, HALO, w)
        b = b.reshape(groups, HALO, w)
        k = 1
        while k < HALO:
            keep = sub >= k
            a_sh = jnp.where(keep, pltpu.roll(a, k, 1), 1.0)
            b_sh = jnp.where(keep, pltpu.roll(b, k, 1), 0.0)
            b = a * b_sh + b
            a = a * a_sh
            k *= 2
        hs = []
        for g in range(groups):
            hg = a[g] * carry + b[g]
            hs.append(hg)
            carry = hg[HALO - 1:HALO, :]
        hs = jnp.concatenate(hs, axis=0)
        o_ref[0, rows, :] = (hs * _gelu_tanh(gate_ref[0, rows, :].astype(F32))).astype(o_ref.dtype)


def _rg_lru(u, conv_w, conv_b, w_ax, b_a, b_x, lam):
    bsz, p, _ = u.shape
    w = LRU_WIDTH

    def full(shape):
        return pl.BlockSpec(shape, lambda b: (0,) * len(shape))

    return pl.pallas_call(
        _lru_kernel,
        grid=(bsz,),
        in_specs=[
            pl.BlockSpec((1, p, w), lambda b: (b, 0, 3)),
            pl.BlockSpec((1, p, w), lambda b: (b, 0, 4)),
            full((LRU_CONV, w)), full((1, w)), full((LRU_BLOCKS, LRU_BLOCK, 2 * LRU_BLOCK)),
            full((1, w)), full((1, w)), full((1, w)),
        ],
        out_specs=pl.BlockSpec((1, p, w), lambda b: (b, 0, 0)),
        out_shape=jax.ShapeDtypeStruct((bsz, p, w), BF16),
        scratch_shapes=[pltpu.VMEM((2 * CHUNK, w), BF16)],
        compiler_params=_params(("parallel",), 48),
        name="rg_lru",
    )(u, u, conv_w, conv_b.reshape(1, w), w_ax, b_a.reshape(1, w), b_x.reshape(1, w), lam.reshape(1, w))


def _final_norm_kernel(h_ref, g_ref, o_ref):
    x = h_ref[0, CHUNK:, :]
    ms = jnp.mean(x * x, axis=-1, keepdims=True)
    o_ref[0] = (x * lax.rsqrt(ms + EPS)) * g_ref[...]


def _final_norm(h, gain):
    bsz, p, d = h.shape
    return pl.pallas_call(
        _final_norm_kernel,
        grid=(bsz,),
        in_specs=[pl.BlockSpec((1, p, d), lambda b: (b, 0, 0)), pl.BlockSpec((1, d), lambda b: (0, 0))],
        out_specs=pl.BlockSpec((1, p - CHUNK, d), lambda b: (b, 0, 0)),
        out_shape=jax.ShapeDtypeStruct((bsz, p - CHUNK, d), F32),
        compiler_params=_params(("parallel",), 48),
        name="final_norm",
    )(h, gain.reshape(1, d))


def kernel(x, meta_tokens, l0_mix_norm, l0_w_in, l0_ssd_conv_w, l0_ssd_conv_b, l0_ssd_dt_bias, l0_ssd_a_log,
           l0_ssd_d, l0_ssd_norm, l0_ret_norm, l0_w_out, l0_ffn_norm, l0_ffn_w_in, l0_ffn_conv_w, l0_ffn_conv_b,
           l0_ffn_w_out, l1_mix_norm, l1_w_in, l1_lru_conv_w, l1_lru_conv_b, l1_lru_wa, l1_lru_ba, l1_lru_wx,
           l1_lru_bx, l1_lru_lambda, l1_w_out, l1_ffn_norm, l1_ffn_w_in, l1_ffn_conv_w, l1_ffn_conv_b,
           l1_ffn_w_out, final_norm):
    bsz, seq, d = x.shape
    p = seq + CHUNK
    assert d == D_MODEL and seq % SB_TILE == 0

    lead = jnp.concatenate([jnp.zeros((PAD_ROWS, d), x.dtype), meta_tokens.astype(x.dtype)], 0)
    h = jnp.concatenate([jnp.broadcast_to(lead[None], (bsz, CHUNK, d)), x], 1)

    dt0 = SSD_INNER + SSD_INNER + 2 * SSD_GROUPS * SSD_STATE
    w_main = jnp.concatenate([l0_w_in[:, :dt0], l0_w_in[:, dt0 + SSD_HEADS:]], 1).astype(BF16)
    w_dt = jnp.pad(l0_w_in[:, dt0:dt0 + SSD_HEADS], ((0, 0), (0, LANE - SSD_HEADS))).astype(BF16)
    u, dt_raw = _norm_matmul(h.reshape(bsz * p, d), l0_mix_norm, w_main, w_dt)
    u = u.reshape(bsz, p, -1)
    y_ssd = _ssd(u, dt_raw.reshape(bsz, p, LANE), l0_ssd_conv_w, l0_ssd_conv_b, l0_ssd_dt_bias, l0_ssd_a_log,
                 l0_ssd_d, l0_ssd_norm)
    y_ret = _retention(u, l0_ret_norm)
    w_out = l0_w_out.astype(BF16)
    h = _mix_ffn(y_ssd, y_ret, w_out[:SSD_INNER], w_out[SSD_INNER:], h, l0_ffn_norm, l0_ffn_w_in.astype(BF16),
                 l0_ffn_conv_w, l0_ffn_conv_b, l0_ffn_w_out.astype(BF16))

    u = _norm_matmul(h.reshape(bsz * p, d), l1_mix_norm, l1_w_in.astype(BF16)).reshape(bsz, p, -1)
    y_sb = _sb_attention(u)
    w_ax = jnp.concatenate([l1_lru_wa, l1_lru_wx], -1).astype(BF16)
    y_lru = _rg_lru(u, l1_lru_conv_w, l1_lru_conv_b, w_ax, l1_lru_ba, l1_lru_bx, l1_lru_lambda)
    w_out = l1_w_out.astype(BF16)
    h = _mix_ffn(y_sb, y_lru, w_out[:SB_WIDTH], w_out[SB_WIDTH:], h, l1_ffn_norm, l1_ffn_w_in.astype(BF16),
                 l1_ffn_conv_w, l1_ffn_conv_b, l1_ffn_w_out.astype(BF16))

    return _final_norm(h, final_norm)
```

```python
import functools
import math

import jax
import jax.numpy as jnp
from jax import lax
from jax.experimental import pallas as pl
from jax.experimental.pallas import tpu as pltpu

F32 = jnp.float32
BF16 = jnp.bfloat16

D_MODEL = 1024
N_META = 16
CHUNK = 128
PAD_ROWS = CHUNK - N_META
EPS = 1e-6

SSD_HEADS = 16
SSD_HEAD_DIM = 64
SSD_INNER = SSD_HEADS * SSD_HEAD_DIM
SSD_GROUPS = 4
SSD_STATE = 128
SSD_CONV = 4
GROUP_W = SSD_INNER // SSD_GROUPS
HEADS_PER_GROUP = SSD_HEADS // SSD_GROUPS

RET_HEADS = 4
RET_DIM = 256

SB_HEADS = 16
SB_HEAD_DIM = 64
SB_WIDTH = SB_HEADS * SB_HEAD_DIM
SB_TILE = 256

LRU_WIDTH = 1024
LRU_BLOCKS = 8
LRU_BLOCK = LRU_WIDTH // LRU_BLOCKS
LRU_CONV = 4
LRU_C = 8.0

FFN_DIM = 2816
FFN_CONV = 3
FFN_SUB_ROWS = 272

LANE = 128
HALO = 8
MIB = 1024 * 1024
LOG2E = 1.4426950408889634


def _params(sem, vmem_mib):
    return pltpu.CompilerParams(dimension_semantics=sem, vmem_limit_bytes=vmem_mib * MIB)


def _row_tile(n, cap, mult=16):
    best = None
    for t in range(mult, min(n, cap) + 1, mult):
        if n % t == 0:
            best = t
    assert best is not None, (n, cap)
    return best


def _sigmoid(x):
    return 1.0 / (1.0 + jnp.exp(-x))


def _silu(x):
    return x * _sigmoid(x)


def _softplus(x):
    return jnp.maximum(x, 0.0) + jnp.log1p(jnp.exp(-jnp.abs(x)))


def _dot(a, b):
    return jnp.dot(a, b, preferred_element_type=F32)


def _dot_nt(a, b):
    return lax.dot_general(a, b, (((1,), (1,)), ((), ())), preferred_element_type=F32)


def _split3_dot(tri_bf16, x):
    x1 = x.astype(BF16)
    r1 = x - x1.astype(F32)
    x2 = r1.astype(BF16)
    x3 = (r1 - x2.astype(F32)).astype(BF16)
    return _dot(tri_bf16, x1) + _dot(tri_bf16, x2) + _dot(tri_bf16, x3)


def _norm_matmul_kernel(x_ref, g_ref, w_ref, *rest, has_aux, tn):
    if has_aux:
        waux_ref, o_ref, aux_ref = rest
    else:
        (o_ref,) = rest
    x = x_ref[...]
    ms = jnp.mean(x * x, axis=-1, keepdims=True)
    xn = ((x * lax.rsqrt(ms + EPS)) * g_ref[...]).astype(BF16)
    if has_aux:
        aux_ref[...] = _dot(xn, waux_ref[...])
    for j in range(o_ref.shape[1] // tn):
        o_ref[:, j * tn:(j + 1) * tn] = _dot(xn, w_ref[:, j * tn:(j + 1) * tn]).astype(o_ref.dtype)


def _resident(shape):
    return pl.BlockSpec(shape, lambda *_: (0,) * len(shape), pipeline_mode=pl.Buffered(1))


def _norm_matmul(h2d, gain, w, w_aux=None):
    m, d = h2d.shape
    n = w.shape[1]
    tm = _row_tile(m, 512)
    tn = 1024
    assert n % tn == 0
    has_aux = w_aux is not None
    in_specs = [pl.BlockSpec((tm, d), lambda i: (i, 0)), _resident((1, d)), _resident((d, n))]
    out_shape = [jax.ShapeDtypeStruct((m, n), BF16)]
    out_specs = [pl.BlockSpec((tm, n), lambda i: (i, 0))]
    args = [h2d, gain.reshape(1, d), w]
    if has_aux:
        in_specs.append(_resident((d, LANE)))
        out_shape.append(jax.ShapeDtypeStruct((m, LANE), F32))
        out_specs.append(pl.BlockSpec((tm, LANE), lambda i: (i, 0)))
        args.append(w_aux)
    outs = pl.pallas_call(
        functools.partial(_norm_matmul_kernel, has_aux=has_aux, tn=tn),
        grid=(m // tm,),
        in_specs=in_specs,
        out_specs=out_specs,
        out_shape=out_shape,
        compiler_params=_params(("parallel",), 48),
        name="norm_matmul_aux" if has_aux else "norm_matmul",
    )(*args)
    return outs if has_aux else outs[0]


def _valid_rows(chunk_idx, rows):
    r = lax.broadcasted_iota(jnp.int32, (rows, 1), 0)
    return jnp.logical_or(chunk_idx > 0, r >= PAD_ROWS)


def _causal_conv(hist_ref, cur16, w_ref, b_ref, taps):
    rows = cur16.shape[0]
    hist_ref[rows:2 * rows, :] = cur16
    both = hist_ref[...]
    t = lax.broadcasted_iota(jnp.int32, (rows, 2 * rows), 0)
    j = lax.broadcasted_iota(jnp.int32, (rows, 2 * rows), 1)
    out = b_ref[...] + w_ref[taps - 1:taps, :] * cur16.astype(F32)
    for k in range(taps - 1):
        back = taps - 1 - k
        shift = (j == t + (rows - back)).astype(BF16)
        out = out + w_ref[k:k + 1, :] * _dot(shift, both)
    hist_ref[0:rows, :] = cur16
    return out


def _rep_heads(cols, g):
    rows = cols.shape[0]
    lane = lax.broadcasted_iota(jnp.int32, (rows, GROUP_W), 1)
    h0 = g * HEADS_PER_GROUP
    out = jnp.broadcast_to(cols[:, h0 + 3:h0 + 4], (rows, GROUP_W))
    for hh in (2, 1, 0):
        out = jnp.where(lane < (hh + 1) * SSD_HEAD_DIM, cols[:, h0 + hh:h0 + hh + 1], out)
    return out


def _ssd_kernel(z_ref, xs_ref, bc_ref, dt_ref, cw_ref, cb_ref, dtb_ref, alog_ref, dskip_ref, ng_ref, o_ref,
                hist, state):
    c = pl.program_id(1)
    L = CHUNK

    @pl.when(c == 0)
    def _():
        hist[0:L, :] = jnp.zeros((L, hist.shape[1]), BF16)
        state[...] = jnp.zeros(state.shape, F32)

    valid = _valid_rows(c, L)
    xbc = _causal_conv(hist, jnp.concatenate([xs_ref[0], bc_ref[0]], axis=-1), cw_ref, cb_ref, SSD_CONV)
    xbc = jnp.where(valid, _silu(xbc), 0.0)
    xs = xbc[:, :SSD_INNER]
    bc = xbc[:, SSD_INNER:]

    dt = jnp.where(valid, _softplus(dt_ref[0] + dtb_ref[...]), 0.0)
    a_dt = dt * (-jnp.exp(alog_ref[...]))
    row = lax.broadcasted_iota(jnp.int32, (L, L), 0)
    col = lax.broadcasted_iota(jnp.int32, (L, L), 1)
    lower = row >= col
    a_col = _split3_dot(lower.astype(BF16), a_dt)
    a_row = a_col.T
    lane = lax.broadcasted_iota(jnp.int32, (L, GROUP_W), 1)
    groups = range(SSD_GROUPS)

    bm = [bc[:, g * SSD_STATE:(g + 1) * SSD_STATE] for g in groups]
    cm16 = [bc[:, (SSD_GROUPS + g) * SSD_STATE:(SSD_GROUPS + g + 1) * SSD_STATE].astype(BF16) for g in groups]
    s_prev = [state[g] for g in groups]
    cb = [_dot_nt(cm16[g], bm[g].astype(BF16)) for g in groups]
    y_off = [_dot(cm16[g], s_prev[g].astype(BF16)) for g in groups]
    a_rep = [_rep_heads(a_col, g) for g in groups]
    xg = [xs[:, g * GROUP_W:(g + 1) * GROUP_W] for g in groups]
    xdt = [xg[g] * _rep_heads(dt, g) for g in groups]
    xdt16 = [x.astype(BF16) for x in xdt]
    scores = []
    for h in range(SSD_HEADS):
        seg = a_col[:, h:h + 1] - a_row[h:h + 1, :]
        decay = jnp.where(lower, jnp.exp(seg), 0.0)
        scores.append((cb[h // HEADS_PER_GROUP] * decay).astype(BF16))
    y_diag = [_dot(scores[h], xdt16[h // HEADS_PER_GROUP]) for h in range(SSD_HEADS)]
    a_last = [a[L - 1:L, :] for a in a_rep]
    xdec = [(xdt[g] * jnp.exp(a_last[g] - a_rep[g])).astype(BF16) for g in groups]
    s_new = [_dot(bm[g].T.astype(BF16), xdec[g]) for g in groups]
    for g in groups:
        sl = slice(g * GROUP_W, (g + 1) * GROUP_W)
        state[g] = s_prev[g] * jnp.exp(a_last[g]) + s_new[g]
        y = y_off[g] * jnp.exp(a_rep[g])
        for hh in range(HEADS_PER_GROUP):
            own = jnp.logical_and(lane >= hh * SSD_HEAD_DIM, lane < (hh + 1) * SSD_HEAD_DIM)
            y = y + jnp.where(own, y_diag[g * HEADS_PER_GROUP + hh], 0.0)
        y = y + xg[g] * dskip_ref[:, sl]
        y = y * _silu(z_ref[0, :, sl].astype(F32))
        y = y * lax.rsqrt(jnp.mean(y * y, axis=-1, keepdims=True) + EPS)
        o_ref[0, :, sl] = (y * ng_ref[:, sl]).astype(o_ref.dtype)


def _ssd(u, dt_raw, conv_w, conv_b, dt_bias, a_log, d_skip, norm_g):
    bsz, p, _ = u.shape
    nc = p // CHUNK
    w = SSD_INNER

    def ublock(idx):
        return pl.BlockSpec((1, CHUNK, w), lambda b, c: (b, c, idx))

    def full(shape):
        return pl.BlockSpec(shape, lambda b, c: (0,) * len(shape))

    pad_heads = LANE - SSD_HEADS
    args = (
        u, u, u, dt_raw,
        conv_w, conv_b.reshape(1, 2 * w),
        jnp.pad(dt_bias, (0, pad_heads)).reshape(1, LANE), jnp.pad(a_log, (0, pad_heads)).reshape(1, LANE),
        jnp.repeat(d_skip, SSD_HEAD_DIM).reshape(1, w), norm_g.reshape(1, w),
    )
    in_specs = [
        ublock(0), ublock(1), ublock(2),
        pl.BlockSpec((1, CHUNK, LANE), lambda b, c: (b, c, 0)),
        full((SSD_CONV, 2 * w)), full((1, 2 * w)),
        full((1, LANE)), full((1, LANE)), full((1, w)), full((1, w)),
    ]
    return pl.pallas_call(
        _ssd_kernel,
        grid=(bsz, nc),
        in_specs=in_specs,
        out_specs=pl.BlockSpec((1, CHUNK, w), lambda b, c: (b, c, 0)),
        out_shape=jax.ShapeDtypeStruct((bsz, p, w), BF16),
        scratch_shapes=[
            pltpu.VMEM((2 * CHUNK, 2 * w), BF16),
            pltpu.VMEM((SSD_GROUPS, SSD_STATE, GROUP_W), F32),
        ],
        compiler_params=_params(("parallel", "arbitrary"), 32),
        name="ssd_scan",
    )(*args)


def _rotate(x, cos, sin):
    half = RET_DIM // 2
    x1, x2 = x[:, :half], x[:, half:]
    return jnp.concatenate([x1 * cos - x2 * sin, x1 * sin + x2 * cos], axis=-1)


def _ret_kernel(q_ref, k_ref, v_ref, g_ref, cos_ref, sin_ref, dec_ref, zeta_ref, xi_ref, cd_ref, ng_ref,
                o_ref, state):
    c = pl.program_id(1)
    L = CHUNK

    @pl.when(c == 0)
    def _():
        state[...] = jnp.zeros(state.shape, F32)

    valid = _valid_rows(c, L)
    cos = cos_ref[...]
    sin = sin_ref[...]
    heads = range(RET_HEADS)
    cols = [slice(h * RET_DIM, (h + 1) * RET_DIM) for h in heads]

    q16 = [jnp.where(valid, _rotate(q_ref[0, :, cols[h]].astype(F32), cos, sin), 0.0).astype(BF16) for h in heads]
    k = [jnp.where(valid, _rotate(k_ref[0, :, cols[h]].astype(F32), cos, sin) * (RET_DIM ** -0.5), 0.0)
         for h in heads]
    v16 = [jnp.where(valid, v_ref[0, :, cols[h]], jnp.zeros((), BF16)) for h in heads]
    r_prev = [state[h] for h in heads]
    qk = [_dot_nt(q16[h], k[h].astype(BF16)) for h in heads]
    cross = [_dot(q16[h], r_prev[h].astype(BF16)) for h in heads]
    scores = [(qk[h] * dec_ref[h]).astype(BF16) for h in heads]
    kz = [(k[h] * zeta_ref[:, h:h + 1]).T.astype(BF16) for h in heads]
    inner = [_dot(scores[h], v16[h]) for h in heads]
    kv = [_dot(kz[h], v16[h]) for h in heads]
    for h in heads:
        state[h] = r_prev[h] * cd_ref[:, h:h + 1] + kv[h]
        o = inner[h] + cross[h] * xi_ref[:, h:h + 1]
        o = o - jnp.mean(o, axis=-1, keepdims=True)
        o = o * lax.rsqrt(jnp.mean(o * o, axis=-1, keepdims=True) + EPS)
        o = o * ng_ref[:, cols[h]]
        o_ref[0, :, cols[h]] = (_silu(g_ref[0, :, cols[h]].astype(F32)) * o).astype(o_ref.dtype)


def _retention(u, norm_g):
    bsz, p, _ = u.shape
    nc = p // CHUNK
    w = RET_HEADS * RET_DIM
    half = RET_DIM // 2
    pos = jnp.arange(p, dtype=F32) - PAD_ROWS
    inv_freq = 1.0 / (10000.0 ** (jnp.arange(half, dtype=F32) / (half - 1)))
    ang = pos[:, None] * inv_freq[None, :]
    log_gamma = jnp.log1p(-jnp.exp2(-5.0 - jnp.arange(RET_HEADS, dtype=F32)))
    idx = jnp.arange(CHUNK, dtype=F32)
    diff = idx[:, None] - idx[None, :]
    decay = jnp.where(diff >= 0, jnp.exp(log_gamma[:, None, None] * jnp.maximum(diff, 0.0)), 0.0)
    pad_heads = LANE - RET_HEADS
    zeta = jnp.pad(jnp.exp(log_gamma[None, :] * (CHUNK - 1 - idx)[:, None]), ((0, 0), (0, pad_heads)))
    xi = jnp.pad(jnp.exp(log_gamma[None, :] * (idx + 1.0)[:, None]), ((0, 0), (0, pad_heads)))
    chunk_decay = jnp.pad(jnp.exp(CHUNK * log_gamma), (0, pad_heads)).reshape(1, LANE)

    def ublock(idx_):
        return pl.BlockSpec((1, CHUNK, w), lambda b, c: (b, c, idx_))

    def full(shape):
        return pl.BlockSpec(shape, lambda b, c: (0,) * len(shape))

    return pl.pallas_call(
        _ret_kernel,
        grid=(bsz, nc),
        in_specs=[
            ublock(3), ublock(4), ublock(5), ublock(6),
            pl.BlockSpec((CHUNK, half), lambda b, c: (c, 0)),
            pl.BlockSpec((CHUNK, half), lambda b, c: (c, 0)),
            full((RET_HEADS, CHUNK, CHUNK)), full((CHUNK, LANE)), full((CHUNK, LANE)), full((1, LANE)),
            full((1, w)),
        ],
        out_specs=pl.BlockSpec((1, CHUNK, w), lambda b, c: (b, c, 0)),
        out_shape=jax.ShapeDtypeStruct((bsz, p, w), BF16),
        scratch_shapes=[pltpu.VMEM((RET_HEADS, RET_DIM, RET_DIM), F32)],
        compiler_params=_params(("parallel", "arbitrary"), 32),
        name="retention",
    )(u, u, u, u, jnp.cos(ang), jnp.sin(ang), decay, zeta, xi, chunk_decay, norm_g.reshape(1, w))


def _mix_ffn_kernel(y1_ref, y2_ref, w1_ref, w2_ref, h_ref, gain_ref, win_ref, cw_ref, cb_ref, wo_ref, o_ref,
                    bufg, bufu):
    t = pl.program_id(1)
    tm = o_ref.shape[1]
    sub = _row_tile(tm, FFN_SUB_ROWS)
    nsub = tm // sub

    @pl.when(t == 0)
    def _():
        bufg[0:HALO, :] = jnp.zeros((HALO, FFN_DIM), F32)
        bufu[0:HALO, :] = jnp.zeros((HALO, FFN_DIM), F32)

    def rows(i):
        return slice(i * sub, (i + 1) * sub)

    def out_proj(i):
        o_ref[0, rows(i), :] = (h_ref[0, rows(i), :] + _dot(y1_ref[0, rows(i), :], w1_ref[...])
                                + _dot(y2_ref[0, rows(i), :], w2_ref[...]))

    def up_proj(i):
        x = o_ref[0, rows(i), :]
        ms = jnp.mean(x * x, axis=-1, keepdims=True)
        xn = ((x * lax.rsqrt(ms + EPS)) * gain_ref[...]).astype(BF16)
        bufg[HALO + i * sub:HALO + (i + 1) * sub, :] = _dot(xn, win_ref[:, :FFN_DIM])
        bufu[HALO + i * sub:HALO + (i + 1) * sub, :] = _dot(xn, win_ref[:, FFN_DIM:])

    def conv(buf, half, i):
        cols = slice(half * FFN_DIM, (half + 1) * FFN_DIM)
        out = cb_ref[:, cols]
        for k in range(FFN_CONV):
            off = HALO - (FFN_CONV - 1) + k + i * sub
            out = out + cw_ref[k:k + 1, cols] * buf[off:off + sub, :]
        return out

    def gate_down(i):
        act = (_silu(conv(bufg, 0, i)) * conv(bufu, 1, i)).astype(BF16)
        out = o_ref[0, rows(i), :] + _dot(act, wo_ref[...])
        r = t * tm + i * sub + lax.broadcasted_iota(jnp.int32, (sub, 1), 0)
        o_ref[0, rows(i), :] = jnp.where(r >= PAD_ROWS, out, 0.0)

    for i in range(nsub):
        out_proj(i)
    for i in range(nsub):
        up_proj(i)
    for i in range(nsub):
        gate_down(i)
    bufg[0:HALO, :] = bufg[tm:tm + HALO, :]
    bufu[0:HALO, :] = bufu[tm:tm + HALO, :]


def _mix_ffn(y1, y2, w1, w2, h, gain, w_in, conv_w, conv_b, w_out):
    bsz, p, d = h.shape
    tm = _row_tile(p, 2 * FFN_SUB_ROWS)
    k1, k2 = y1.shape[-1], y2.shape[-1]

    def tile(width):
        return pl.BlockSpec((1, tm, width), lambda b, t: (b, t, 0))

    return pl.pallas_call(
        _mix_ffn_kernel,
        grid=(bsz, p // tm),
        in_specs=[
            tile(k1), tile(k2), _resident((k1, d)), _resident((k2, d)), tile(d), _resident((1, d)),
            _resident((d, 2 * FFN_DIM)), _resident((FFN_CONV, 2 * FFN_DIM)), _resident((1, 2 * FFN_DIM)),
            _resident((FFN_DIM, d)),
        ],
        out_specs=tile(d),
        out_shape=jax.ShapeDtypeStruct((bsz, p, d), F32),
        scratch_shapes=[pltpu.VMEM((HALO + tm, FFN_DIM), F32), pltpu.VMEM((HALO + tm, FFN_DIM), F32)],
        compiler_params=_params(("parallel", "arbitrary"), 60),
        name="mix_ffn",
    )(y1, y2, w1, w2, h, gain.reshape(1, d), w_in, conv_w, conv_b.reshape(1, 2 * FFN_DIM), w_out)


def _sb_kernel(q_ref, k_ref, v_ref, o_ref, qs_ref, *, nq):
    tk = SB_TILE
    tq = CHUNK
    p = q_ref.shape[1]

    def iotas(rows, cols):
        return lax.broadcasted_iota(jnp.int32, (rows, cols), 0), lax.broadcasted_iota(jnp.int32, (rows, cols), 1)

    def newer_keys(n):
        r, c = iotas(n, n)
        return (r > c).astype(BF16)

    upper_k = newer_keys(tk)
    upper_c = newer_keys(CHUNK)
    rr, cc = iotas(tq, CHUNK)
    strict = cc < rr
    rr2, cc2 = iotas(tq, tk)
    strict_second = cc2 < rr2 + CHUNK
    meta_key = cc >= PAD_ROWS
    meta_mask = jnp.logical_and(strict, meta_key)

    def first_head(rows):
        return lax.broadcasted_iota(jnp.int32, (rows, LANE), 1) < SB_HEAD_DIM

    qf = q_ref[0].astype(F32) * (SB_HEAD_DIM ** -0.5)
    qs_ref[0] = jnp.where(first_head(p), qf, 0.0).astype(BF16)
    qs_ref[1] = jnp.where(first_head(p), 0.0, qf).astype(BF16)

    def q_rows(i):
        return slice(CHUNK + i * tq, CHUNK + (i + 1) * tq)

    def k_rows(j):
        return slice(CHUNK + j * tk, CHUNK + (j + 1) * tk)

    meta_rows = slice(0, CHUNK)
    tiles = [(meta_rows, meta_rows, meta_mask, upper_c, "only")]
    for i in range(nq):
        own = i // 2
        if i % 2 == 0:
            tiles.append((q_rows(i), q_rows(i), strict, upper_c, "first"))
        else:
            tiles.append((q_rows(i), k_rows(own), strict_second, upper_k, "first"))
        tiles += [(q_rows(i), k_rows(j), None, upper_k, "middle") for j in range(own - 1, -1, -1)]
        tiles.append((q_rows(i), meta_rows, meta_key, upper_c, "last"))
    heads = range(2)
    scored, weighted = {}, {}
    acc, car = [None, None], [None, None]

    def scores(t):
        qrows, krows, mask, _, _ = tiles[t]
        k16 = k_ref[0, krows, :]
        zs = [_dot_nt(qs_ref[hd, qrows, :], k16) for hd in heads]
        for hd in heads:
            z = zs[hd]
            sp = jnp.maximum(z, 0.0) + jnp.log(1.0 + jnp.exp2(jnp.abs(z) * (-LOG2E)))
            if mask is not None:
                sp = jnp.where(mask, sp, 0.0)
            scored[t, hd] = (sp.astype(BF16), z - sp, sp[:, 0:1])

    def weights(t):
        _, _, mask, upper, _ = tiles[t]
        newer = [_dot(scored[t, hd][0], upper) for hd in heads]
        for hd in heads:
            _, logsig, sp0 = scored.pop((t, hd))
            w = jnp.exp(logsig - newer[hd])
            if mask is not None:
                w = jnp.where(mask, w, 0.0)
            weighted[t, hd] = (w.astype(BF16), jnp.broadcast_to(newer[hd][:, 0:1] + sp0, (w.shape[0], LANE)))

    def products(t):
        qrows, krows, _, _, role = tiles[t]
        v16 = v_ref[0, krows, :]
        pvs = [_dot(weighted[t, hd][0], v16) for hd in heads]
        for hd in heads:
            _, rowsum = weighted.pop((t, hd))
            if role in ("only", "first"):
                acc[hd], car[hd] = pvs[hd], rowsum
            else:
                acc[hd] = acc[hd] + pvs[hd] * jnp.exp(-car[hd])
                car[hd] = car[hd] + rowsum
        if role in ("only", "last"):
            o_ref[0, qrows, :] = jnp.where(first_head(acc[0].shape[0]), acc[0], acc[1]).astype(o_ref.dtype)

    for t in range(len(tiles) + 2):
        if t < len(tiles):
            scores(t)
        if 1 <= t <= len(tiles):
            weights(t - 1)
        if t >= 2:
            products(t - 2)


def _sb_attention(u):
    bsz, p, _ = u.shape
    pairs = SB_WIDTH // LANE
    nq = (p - CHUNK) // CHUNK
    assert (p - CHUNK) % SB_TILE == 0

    def ublock(base):
        return pl.BlockSpec((1, p, LANE), lambda b, hp: (b, 0, base + hp))

    return pl.pallas_call(
        functools.partial(_sb_kernel, nq=nq),
        grid=(bsz, pairs),
        in_specs=[ublock(0), ublock(pairs), ublock(2 * pairs)],
        out_specs=pl.BlockSpec((1, p, LANE), lambda b, hp: (b, 0, hp)),
        out_shape=jax.ShapeDtypeStruct((bsz, p, SB_WIDTH), BF16),
        scratch_shapes=[pltpu.VMEM((2, p, LANE), BF16)],
        compiler_params=_params(("parallel", "parallel"), 48),
        name="stickbreak_attention",
    )(u, u, u)


def _gelu_tanh(x):
    return 0.5 * x * (1.0 + jnp.tanh(math.sqrt(2.0 / math.pi) * (x + 0.044715 * (x * x * x))))


def _lru_kernel(gate_ref, xr_ref, cw_ref, cb_ref, wax_ref, ba_ref, bx_ref, lam_ref, o_ref, hist):
    L = CHUNK
    w = LRU_WIDTH
    tiles = o_ref.shape[1] // L
    hist[0:L, :] = jnp.zeros((L, w), BF16)
    soft_lam = _softplus(-lam_ref[...])
    sub = lax.broadcasted_iota(jnp.int32, (1, HALO, 1), 1)
    groups = L // HALO
    carry = jnp.zeros((1, w), F32)
    for t in range(tiles):
        rows = slice(t * L, (t + 1) * L)
        x = _causal_conv(hist, xr_ref[0, rows, :], cw_ref, cb_ref, LRU_CONV)
        x16 = x.astype(BF16)
        gates = [_dot(x16[:, n * LRU_BLOCK:(n + 1) * LRU_BLOCK], wax_ref[n]) for n in range(LRU_BLOCKS)]
        r = _sigmoid(jnp.concatenate([g[:, :LRU_BLOCK] for g in gates], axis=-1) + ba_ref[...])
        i = _sigmoid(jnp.concatenate([g[:, LRU_BLOCK:] for g in gates], axis=-1) + bx_ref[...])
        log_a = -LRU_C * r * soft_lam
        a = jnp.exp(log_a)
        gap = 1.0 - jnp.exp(2.0 * log_a)
        b = jnp.where(gap > 0.0, gap * lax.rsqrt(gap), 0.0) * (i * x)
        if t == 0:
            b = jnp.where(lax.broadcasted_iota(jnp.int32, (L, 1), 0) >= PAD_ROWS, b, 0.0)

        a = a.reshape(groups, HALO, w)
        b = b.reshape(groups, HALO, w)
        k = 1
        while k < HALO:
            keep = sub >= k
            a_sh = jnp.where(keep, pltpu.roll(a, k, 1), 1.0)
            b_sh = jnp.where(keep, pltpu.roll(b, k, 1), 0.0)
            b = a * b_sh + b
            a = a * a_sh
            k *= 2
        hs = []
        for g in range(groups):
            hg = a[g] * carry + b[g]
            hs.append(hg)
            carry = hg[HALO - 1:HALO, :]
        hs = jnp.concatenate(hs, axis=0)
        o_ref[0, rows, :] = (hs * _gelu_tanh(gate_ref[0, rows, :].astype(F32))).astype(o_ref.dtype)


def _rg_lru(u, conv_w, conv_b, w_ax, b_a, b_x, lam):
    bsz, p, _ = u.shape
    w = LRU_WIDTH

    def full(shape):
        return pl.BlockSpec(shape, lambda b: (0,) * len(shape))

    return pl.pallas_call(
        _lru_kernel,
        grid=(bsz,),
        in_specs=[
            pl.BlockSpec((1, p, w), lambda b: (b, 0, 3)),
            pl.BlockSpec((1, p, w), lambda b: (b, 0, 4)),
            full((LRU_CONV, w)), full((1, w)), full((LRU_BLOCKS, LRU_BLOCK, 2 * LRU_BLOCK)),
            full((1, w)), full((1, w)), full((1, w)),
        ],
        out_specs=pl.BlockSpec((1, p, w), lambda b: (b, 0, 0)),
        out_shape=jax.ShapeDtypeStruct((bsz, p, w), BF16),
        scratch_shapes=[pltpu.VMEM((2 * CHUNK, w), BF16)],
        compiler_params=_params(("parallel",), 48),
        name="rg_lru",
    )(u, u, conv_w, conv_b.reshape(1, w), w_ax, b_a.reshape(1, w), b_x.reshape(1, w), lam.reshape(1, w))


def _final_norm_kernel(h_ref, g_ref, o_ref):
    x = h_ref[0, CHUNK:, :]
    ms = jnp.mean(x * x, axis=-1, keepdims=True)
    o_ref[0] = (x * lax.rsqrt(ms + EPS)) * g_ref[...]


def _final_norm(h, gain):
    bsz, p, d = h.shape
    return pl.pallas_call(
        _final_norm_kernel,
        grid=(bsz,),
        in_specs=[pl.BlockSpec((1, p, d), lambda b: (b, 0, 0)), pl.BlockSpec((1, d), lambda b: (0, 0))],
        out_specs=pl.BlockSpec((1, p - CHUNK, d), lambda b: (b, 0, 0)),
        out_shape=jax.ShapeDtypeStruct((bsz, p - CHUNK, d), F32),
        compiler_params=_params(("parallel",), 48),
        name="final_norm",
    )(h, gain.reshape(1, d))


def kernel(x, meta_tokens, l0_mix_norm, l0_w_in, l0_ssd_conv_w, l0_ssd_conv_b, l0_ssd_dt_bias, l0_ssd_a_log,
           l0_ssd_d, l0_ssd_norm, l0_ret_norm, l0_w_out, l0_ffn_norm, l0_ffn_w_in, l0_ffn_conv_w, l0_ffn_conv_b,
           l0_ffn_w_out, l1_mix_norm, l1_w_in, l1_lru_conv_w, l1_lru_conv_b, l1_lru_wa, l1_lru_ba, l1_lru_wx,
           l1_lru_bx, l1_lru_lambda, l1_w_out, l1_ffn_norm, l1_ffn_w_in, l1_ffn_conv_w, l1_ffn_conv_b,
           l1_ffn_w_out, final_norm):
    bsz, seq, d = x.shape
    p = seq + CHUNK
    assert d == D_MODEL and seq % SB_TILE == 0

    lead = jnp.concatenate([jnp.zeros((PAD_ROWS, d), x.dtype), meta_tokens.astype(x.dtype)], 0)
    h = jnp.concatenate([jnp.broadcast_to(lead[None], (bsz, CHUNK, d)), x], 1)

    dt0 = SSD_INNER + SSD_INNER + 2 * SSD_GROUPS * SSD_STATE
    w_main = jnp.concatenate([l0_w_in[:, :dt0], l0_w_in[:, dt0 + SSD_HEADS:]], 1).astype(BF16)
    w_dt = jnp.pad(l0_w_in[:, dt0:dt0 + SSD_HEADS], ((0, 0), (0, LANE - SSD_HEADS))).astype(BF16)
    u, dt_raw = _norm_matmul(h.reshape(bsz * p, d), l0_mix_norm, w_main, w_dt)
    u = u.reshape(bsz, p, -1)
    y_ssd = _ssd(u, dt_raw.reshape(bsz, p, LANE), l0_ssd_conv_w, l0_ssd_conv_b, l0_ssd_dt_bias, l0_ssd_a_log,
                 l0_ssd_d, l0_ssd_norm)
    y_ret = _retention(u, l0_ret_norm)
    w_out = l0_w_out.astype(BF16)
    h = _mix_ffn(y_ssd, y_ret, w_out[:SSD_INNER], w_out[SSD_INNER:], h, l0_ffn_norm, l0_ffn_w_in.astype(BF16),
                 l0_ffn_conv_w, l0_ffn_conv_b, l0_ffn_w_out.astype(BF16))

    u = _norm_matmul(h.reshape(bsz * p, d), l1_mix_norm, l1_w_in.astype(BF16)).reshape(bsz, p, -1)
    y_sb = _sb_attention(u)
    w_ax = jnp.concatenate([l1_lru_wa, l1_lru_wx], -1).astype(BF16)
    y_lru = _rg_lru(u, l1_lru_conv_w, l1_lru_conv_b, w_ax, l1_lru_ba, l1_lru_bx, l1_lru_lambda)
    w_out = l1_w_out.astype(BF16)
    h = _mix_ffn(y_sb, y_lru, w_out[:SB_WIDTH], w_out[SB_WIDTH:], h, l1_ffn_norm, l1_ffn_w_in.astype(BF16),
                 l1_ffn_conv_w, l1_ffn_conv_b, l1_ffn_w_out.astype(BF16))

    return _final_norm(h, final_norm)
```

```python
import functools
import math

import jax
import jax.numpy as jnp
from jax import lax
from jax.experimental import pallas as pl
from jax.experimental.pallas import tpu as pltpu

F32 = jnp.float32
BF16 = jnp.bfloat16

D_MODEL = 1024
N_META = 16
CHUNK = 128
PAD_ROWS = CHUNK - N_META
EPS = 1e-6

SSD_HEADS = 16
SSD_HEAD_DIM = 64
SSD_INNER = SSD_HEADS * SSD_HEAD_DIM
SSD_GROUPS = 4
SSD_STATE = 128
SSD_CONV = 4
GROUP_W = SSD_INNER // SSD_GROUPS
HEADS_PER_GROUP = SSD_HEADS // SSD_GROUPS

RET_HEADS = 4
RET_DIM = 256

SB_HEADS = 16
SB_HEAD_DIM = 64
SB_WIDTH = SB_HEADS * SB_HEAD_DIM
SB_TILE = 256

LRU_WIDTH = 1024
LRU_BLOCKS = 8
LRU_BLOCK = LRU_WIDTH // LRU_BLOCKS
LRU_CONV = 4
LRU_C = 8.0

FFN_DIM = 2816
FFN_CONV = 3
FFN_SUB_ROWS = 272

LANE = 128
HALO = 8
MIB = 1024 * 1024
LOG2E = 1.4426950408889634


def _params(sem, vmem_mib):
    return pltpu.CompilerParams(dimension_semantics=sem, vmem_limit_bytes=vmem_mib * MIB)


def _row_tile(n, cap, mult=16):
    best = None
    for t in range(mult, min(n, cap) + 1, mult):
        if n % t == 0:
            best = t
    assert best is not None, (n, cap)
    return best


def _sigmoid(x):
    return 1.0 / (1.0 + jnp.exp(-x))


def _silu(x):
    return x * _sigmoid(x)


def _softplus(x):
    return jnp.maximum(x, 0.0) + jnp.log1p(jnp.exp(-jnp.abs(x)))


def _dot(a, b):
    return jnp.dot(a, b, preferred_element_type=F32)


def _dot_nt(a, b):
    return lax.dot_general(a, b, (((1,), (1,)), ((), ())), preferred_element_type=F32)


def _split3_dot(tri_bf16, x):
    x1 = x.astype(BF16)
    r1 = x - x1.astype(F32)
    x2 = r1.astype(BF16)
    x3 = (r1 - x2.astype(F32)).astype(BF16)
    return _dot(tri_bf16, x1) + _dot(tri_bf16, x2) + _dot(tri_bf16, x3)


def _norm_matmul_kernel(x_ref, g_ref, w_ref, *rest, has_aux, tn):
    if has_aux:
        waux_ref, o_ref, aux_ref = rest
    else:
        (o_ref,) = rest
    x = x_ref[...]
    ms = jnp.mean(x * x, axis=-1, keepdims=True)
    xn = ((x * lax.rsqrt(ms + EPS)) * g_ref[...]).astype(BF16)
    if has_aux:
        aux_ref[...] = _dot(xn, waux_ref[...])
    for j in range(o_ref.shape[1] // tn):
        o_ref[:, j * tn:(j + 1) * tn] = _dot(xn, w_ref[:, j * tn:(j + 1) * tn]).astype(o_ref.dtype)


def _resident(shape):
    return pl.BlockSpec(shape, lambda *_: (0,) * len(shape), pipeline_mode=pl.Buffered(1))


def _norm_matmul(h2d, gain, w, w_aux=None):
    m, d = h2d.shape
    n = w.shape[1]
    tm = _row_tile(m, 512)
    tn = 1024
    assert n % tn == 0
    has_aux = w_aux is not None
    in_specs = [pl.BlockSpec((tm, d), lambda i: (i, 0)), _resident((1, d)), _resident((d, n))]
    out_shape = [jax.ShapeDtypeStruct((m, n), BF16)]
    out_specs = [pl.BlockSpec((tm, n), lambda i: (i, 0))]
    args = [h2d, gain.reshape(1, d), w]
    if has_aux:
        in_specs.append(_resident((d, LANE)))
        out_shape.append(jax.ShapeDtypeStruct((m, LANE), F32))
        out_specs.append(pl.BlockSpec((tm, LANE), lambda i: (i, 0)))
        args.append(w_aux)
    outs = pl.pallas_call(
        functools.partial(_norm_matmul_kernel, has_aux=has_aux, tn=tn),
        grid=(m // tm,),
        in_specs=in_specs,
        out_specs=out_specs,
        out_shape=out_shape,
        compiler_params=_params(("parallel",), 48),
        name="norm_matmul_aux" if has_aux else "norm_matmul",
    )(*args)
    return outs if has_aux else outs[0]


def _valid_rows(chunk_idx, rows):
    r = lax.broadcasted_iota(jnp.int32, (rows, 1), 0)
    return jnp.logical_or(chunk_idx > 0, r >= PAD_ROWS)


def _causal_conv(hist_ref, cur16, w_ref, b_ref, taps):
    rows = cur16.shape[0]
    hist_ref[rows:2 * rows, :] = cur16
    both = hist_ref[...]
    t = lax.broadcasted_iota(jnp.int32, (rows, 2 * rows), 0)
    j = lax.broadcasted_iota(jnp.int32, (rows, 2 * rows), 1)
    out = b_ref[...] + w_ref[taps - 1:taps, :] * cur16.astype(F32)
    for k in range(taps - 1):
        back = taps - 1 - k
        shift = (j == t + (rows - back)).astype(BF16)
        out = out + w_ref[k:k + 1, :] * _dot(shift, both)
    hist_ref[0:rows, :] = cur16
    return out


def _rep_heads(cols, g):
    rows = cols.shape[0]
    lane = lax.broadcasted_iota(jnp.int32, (rows, GROUP_W), 1)
    h0 = g * HEADS_PER_GROUP
    out = jnp.broadcast_to(cols[:, h0 + 3:h0 + 4], (rows, GROUP_W))
    for hh in (2, 1, 0):
        out = jnp.where(lane < (hh + 1) * SSD_HEAD_DIM, cols[:, h0 + hh:h0 + hh + 1], out)
    return out


def _ssd_kernel(z_ref, xs_ref, bc_ref, dt_ref, cw_ref, cb_ref, dtb_ref, alog_ref, dskip_ref, ng_ref, o_ref,
                hist, state):
    c = pl.program_id(1)
    L = CHUNK

    @pl.when(c == 0)
    def _():
        hist[0:L, :] = jnp.zeros((L, hist.shape[1]), BF16)
        state[...] = jnp.zeros(state.shape, F32)

    valid = _valid_rows(c, L)
    xbc = _causal_conv(hist, jnp.concatenate([xs_ref[0], bc_ref[0]], axis=-1), cw_ref, cb_ref, SSD_CONV)
    xbc = jnp.where(valid, _silu(xbc), 0.0)
    xs = xbc[:, :SSD_INNER]
    bc = xbc[:, SSD_INNER:]

    dt = jnp.where(valid, _softplus(dt_ref[0] + dtb_ref[...]), 0.0)
    a_dt = dt * (-jnp.exp(alog_ref[...]))
    row = lax.broadcasted_iota(jnp.int32, (L, L), 0)
    col = lax.broadcasted_iota(jnp.int32, (L, L), 1)
    lower = row >= col
    a_col = _split3_dot(lower.astype(BF16), a_dt)
    a_row = a_col.T
    lane = lax.broadcasted_iota(jnp.int32, (L, GROUP_W), 1)
    groups = range(SSD_GROUPS)

    bm = [bc[:, g * SSD_STATE:(g + 1) * SSD_STATE] for g in groups]
    cm16 = [bc[:, (SSD_GROUPS + g) * SSD_STATE:(SSD_GROUPS + g + 1) * SSD_STATE].astype(BF16) for g in groups]
    s_prev = [state[g] for g in groups]
    cb = [_dot_nt(cm16[g], bm[g].astype(BF16)) for g in groups]
    y_off = [_dot(cm16[g], s_prev[g].astype(BF16)) for g in groups]
    a_rep = [_rep_heads(a_col, g) for g in groups]
    xg = [xs[:, g * GROUP_W:(g + 1) * GROUP_W] for g in groups]
    xdt = [xg[g] * _rep_heads(dt, g) for g in groups]
    xdt16 = [x.astype(BF16) for x in xdt]
    scores = []
    for h in range(SSD_HEADS):
        seg = a_col[:, h:h + 1] - a_row[h:h + 1, :]
        decay = jnp.where(lower, jnp.exp(seg), 0.0)
        scores.append((cb[h // HEADS_PER_GROUP] * decay).astype(BF16))
    y_diag = [_dot(scores[h], xdt16[h // HEADS_PER_GROUP]) for h in range(SSD_HEADS)]
    a_last = [a[L - 1:L, :] for a in a_rep]
    xdec = [(xdt[g] * jnp.exp(a_last[g] - a_rep[g])).astype(BF16) for g in groups]
    s_new = [_dot(bm[g].T.astype(BF16), xdec[g]) for g in groups]
    for g in groups:
        sl = slice(g * GROUP_W, (g + 1) * GROUP_W)
        state[g] = s_prev[g] * jnp.exp(a_last[g]) + s_new[g]
        y = y_off[g] * jnp.exp(a_rep[g])
        for hh in range(HEADS_PER_GROUP):
            own = jnp.logical_and(lane >= hh * SSD_HEAD_DIM, lane < (hh + 1) * SSD_HEAD_DIM)
            y = y + jnp.where(own, y_diag[g * HEADS_PER_GROUP + hh], 0.0)
        y = y + xg[g] * dskip_ref[:, sl]
        y = y * _silu(z_ref[0, :, sl].astype(F32))
        y = y * lax.rsqrt(jnp.mean(y * y, axis=-1, keepdims=True) + EPS)
        o_ref[0, :, sl] = (y * ng_ref[:, sl]).astype(o_ref.dtype)


def _ssd(u, dt_raw, conv_w, conv_b, dt_bias, a_log, d_skip, norm_g):
    bsz, p, _ = u.shape
    nc = p // CHUNK
    w = SSD_INNER

    def ublock(idx):
        return pl.BlockSpec((1, CHUNK, w), lambda b, c: (b, c, idx))

    def full(shape):
        return pl.BlockSpec(shape, lambda b, c: (0,) * len(shape))

    pad_heads = LANE - SSD_HEADS
    args = (
        u, u, u, dt_raw,
        conv_w, conv_b.reshape(1, 2 * w),
        jnp.pad(dt_bias, (0, pad_heads)).reshape(1, LANE), jnp.pad(a_log, (0, pad_heads)).reshape(1, LANE),
        jnp.repeat(d_skip, SSD_HEAD_DIM).reshape(1, w), norm_g.reshape(1, w),
    )
    in_specs = [
        ublock(0), ublock(1), ublock(2),
        pl.BlockSpec((1, CHUNK, LANE), lambda b, c: (b, c, 0)),
        full((SSD_CONV, 2 * w)), full((1, 2 * w)),
        full((1, LANE)), full((1, LANE)), full((1, w)), full((1, w)),
    ]
    return pl.pallas_call(
        _ssd_kernel,
        grid=(bsz, nc),
        in_specs=in_specs,
        out_specs=pl.BlockSpec((1, CHUNK, w), lambda b, c: (b, c, 0)),
        out_shape=jax.ShapeDtypeStruct((bsz, p, w), BF16),
        scratch_shapes=[
            pltpu.VMEM((2 * CHUNK, 2 * w), BF16),
            pltpu.VMEM((SSD_GROUPS, SSD_STATE, GROUP_W), F32),
        ],
        compiler_params=_params(("parallel", "arbitrary"), 32),
        name="ssd_scan",
    )(*args)


def _rotate(x, cos, sin):
    half = RET_DIM // 2
    x1, x2 = x[:, :half], x[:, half:]
    return jnp.concatenate([x1 * cos - x2 * sin, x1 * sin + x2 * cos], axis=-1)


def _ret_kernel(q_ref, k_ref, v_ref, g_ref, cos_ref, sin_ref, dec_ref, zeta_ref, xi_ref, cd_ref, ng_ref,
                o_ref, state):
    c = pl.program_id(1)
    L = CHUNK

    @pl.when(c == 0)
    def _():
        state[...] = jnp.zeros(state.shape, F32)

    valid = _valid_rows(c, L)
    cos = cos_ref[...]
    sin = sin_ref[...]
    heads = range(RET_HEADS)
    cols = [slice(h * RET_DIM, (h + 1) * RET_DIM) for h in heads]

    q16 = [jnp.where(valid, _rotate(q_ref[0, :, cols[h]].astype(F32), cos, sin), 0.0).astype(BF16) for h in heads]
    k = [jnp.where(valid, _rotate(k_ref[0, :, cols[h]].astype(F32), cos, sin) * (RET_DIM ** -0.5), 0.0)
         for h in heads]
    v16 = [jnp.where(valid, v_ref[0, :, cols[h]], jnp.zeros((), BF16)) for h in heads]
    r_prev = [state[h] for h in heads]
    qk = [_dot_nt(q16[h], k[h].astype(BF16)) for h in heads]
    cross = [_dot(q16[h], r_prev[h].astype(BF16)) for h in heads]
    scores = [(qk[h] * dec_ref[h]).astype(BF16) for h in heads]
    kz = [(k[h] * zeta_ref[:, h:h + 1]).T.astype(BF16) for h in heads]
    inner = [_dot(scores[h], v16[h]) for h in heads]
    kv = [_dot(kz[h], v16[h]) for h in heads]
    for h in heads:
        state[h] = r_prev[h] * cd_ref[:, h:h + 1] + kv[h]
        o = inner[h] + cross[h] * xi_ref[:, h:h + 1]
        o = o - jnp.mean(o, axis=-1, keepdims=True)
        o = o * lax.rsqrt(jnp.mean(o * o, axis=-1, keepdims=True) + EPS)
        o = o * ng_ref[:, cols[h]]
        o_ref[0, :, cols[h]] = (_silu(g_ref[0, :, cols[h]].astype(F32)) * o).astype(o_ref.dtype)


def _retention(u, norm_g):
    bsz, p, _ = u.shape
    nc = p // CHUNK
    w = RET_HEADS * RET_DIM
    half = RET_DIM // 2
    pos = jnp.arange(p, dtype=F32) - PAD_ROWS
    inv_freq = 1.0 / (10000.0 ** (jnp.arange(half, dtype=F32) / (half - 1)))
    ang = pos[:, None] * inv_freq[None, :]
    log_gamma = jnp.log1p(-jnp.exp2(-5.0 - jnp.arange(RET_HEADS, dtype=F32)))
    idx = jnp.arange(CHUNK, dtype=F32)
    diff = idx[:, None] - idx[None, :]
    decay = jnp.where(diff >= 0, jnp.exp(log_gamma[:, None, None] * jnp.maximum(diff, 0.0)), 0.0)
    pad_heads = LANE - RET_HEADS
    zeta = jnp.pad(jnp.exp(log_gamma[None, :] * (CHUNK - 1 - idx)[:, None]), ((0, 0), (0, pad_heads)))
    xi = jnp.pad(jnp.exp(log_gamma[None, :] * (idx + 1.0)[:, None]), ((0, 0), (0, pad_heads)))
    chunk_decay = jnp.pad(jnp.exp(CHUNK * log_gamma), (0, pad_heads)).reshape(1, LANE)

    def ublock(idx_):
        return pl.BlockSpec((1, CHUNK, w), lambda b, c: (b, c, idx_))

    def full(shape):
        return pl.BlockSpec(shape, lambda b, c: (0,) * len(shape))

    return pl.pallas_call(
        _ret_kernel,
        grid=(bsz, nc),
        in_specs=[
            ublock(3), ublock(4), ublock(5), ublock(6),
            pl.BlockSpec((CHUNK, half), lambda b, c: (c, 0)),
            pl.BlockSpec((CHUNK, half), lambda b, c: (c, 0)),
            full((RET_HEADS, CHUNK, CHUNK)), full((CHUNK, LANE)), full((CHUNK, LANE)), full((1, LANE)),
            full((1, w)),
        ],
        out_specs=pl.BlockSpec((1, CHUNK, w), lambda b, c: (b, c, 0)),
        out_shape=jax.ShapeDtypeStruct((bsz, p, w), BF16),
        scratch_shapes=[pltpu.VMEM((RET_HEADS, RET_DIM, RET_DIM), F32)],
        compiler_params=_params(("parallel", "arbitrary"), 32),
        name="retention",
    )(u, u, u, u, jnp.cos(ang), jnp.sin(ang), decay, zeta, xi, chunk_decay, norm_g.reshape(1, w))


def _mix_ffn_kernel(y1_ref, y2_ref, w1_ref, w2_ref, h_ref, gain_ref, win_ref, cw_ref, cb_ref, wo_ref, o_ref,
                    bufg, bufu):
    t = pl.program_id(1)
    tm = o_ref.shape[1]
    sub = _row_tile(tm, FFN_SUB_ROWS)
    nsub = tm // sub

    @pl.when(t == 0)
    def _():
        bufg[0:HALO, :] = jnp.zeros((HALO, FFN_DIM), F32)
        bufu[0:HALO, :] = jnp.zeros((HALO, FFN_DIM), F32)

    def rows(i):
        return slice(i * sub, (i + 1) * sub)

    def out_proj(i):
        o_ref[0, rows(i), :] = (h_ref[0, rows(i), :] + _dot(y1_ref[0, rows(i), :], w1_ref[...])
                                + _dot(y2_ref[0, rows(i), :], w2_ref[...]))

    def up_proj(i):
        x = o_ref[0, rows(i), :]
        ms = jnp.mean(x * x, axis=-1, keepdims=True)
        xn = ((x * lax.rsqrt(ms + EPS)) * gain_ref[...]).astype(BF16)
        bufg[HALO + i * sub:HALO + (i + 1) * sub, :] = _dot(xn, win_ref[:, :FFN_DIM])
        bufu[HALO + i * sub:HALO + (i + 1) * sub, :] = _dot(xn, win_ref[:, FFN_DIM:])

    def conv(buf, half, i):
        cols = slice(half * FFN_DIM, (half + 1) * FFN_DIM)
        out = cb_ref[:, cols]
        for k in range(FFN_CONV):
            off = HALO - (FFN_CONV - 1) + k + i * sub
            out = out + cw_ref[k:k + 1, cols] * buf[off:off + sub, :]
        return out

    def gate_down(i):
        act = (_silu(conv(bufg, 0, i)) * conv(bufu, 1, i)).astype(BF16)
        out = o_ref[0, rows(i), :] + _dot(act, wo_ref[...])
        r = t * tm + i * sub + lax.broadcasted_iota(jnp.int32, (sub, 1), 0)
        o_ref[0, rows(i), :] = jnp.where(r >= PAD_ROWS, out, 0.0)

    for i in range(nsub):
        out_proj(i)
    for i in range(nsub):
        up_proj(i)
    for i in range(nsub):
        gate_down(i)
    bufg[0:HALO, :] = bufg[tm:tm + HALO, :]
    bufu[0:HALO, :] = bufu[tm:tm + HALO, :]


def _mix_ffn(y1, y2, w1, w2, h, gain, w_in, conv_w, conv_b, w_out):
    bsz, p, d = h.shape
    tm = _row_tile(p, 2 * FFN_SUB_ROWS)
    k1, k2 = y1.shape[-1], y2.shape[-1]

    def tile(width):
        return pl.BlockSpec((1, tm, width), lambda b, t: (b, t, 0))

    return pl.pallas_call(
        _mix_ffn_kernel,
        grid=(bsz, p // tm),
        in_specs=[
            tile(k1), tile(k2), _resident((k1, d)), _resident((k2, d)), tile(d), _resident((1, d)),
            _resident((d, 2 * FFN_DIM)), _resident((FFN_CONV, 2 * FFN_DIM)), _resident((1, 2 * FFN_DIM)),
            _resident((FFN_DIM, d)),
        ],
        out_specs=tile(d),
        out_shape=jax.ShapeDtypeStruct((bsz, p, d), F32),
        scratch_shapes=[pltpu.VMEM((HALO + tm, FFN_DIM), F32), pltpu.VMEM((HALO + tm, FFN_DIM), F32)],
        compiler_params=_params(("parallel", "arbitrary"), 60),
        name="mix_ffn",
    )(y1, y2, w1, w2, h, gain.reshape(1, d), w_in, conv_w, conv_b.reshape(1, 2 * FFN_DIM), w_out)


def _sb_kernel(q_ref, k_ref, v_ref, o_ref, qs_ref, *, nq):
    tk = SB_TILE
    tq = CHUNK
    p = q_ref.shape[1]

    def iotas(rows, cols):
        return lax.broadcasted_iota(jnp.int32, (rows, cols), 0), lax.broadcasted_iota(jnp.int32, (rows, cols), 1)

    def newer_keys(n):
        r, c = iotas(n, n)
        return (r > c).astype(BF16)

    upper_k = newer_keys(tk)
    upper_c = newer_keys(CHUNK)
    rr, cc = iotas(tq, CHUNK)
    strict = cc < rr
    rr2, cc2 = iotas(tq, tk)
    strict_second = cc2 < rr2 + CHUNK
    meta_mask = jnp.logical_and(strict, cc >= PAD_ROWS)

    def first_head(rows):
        return lax.broadcasted_iota(jnp.int32, (rows, LANE), 1) < SB_HEAD_DIM

    qf = q_ref[0].astype(F32) * (SB_HEAD_DIM ** -0.5)
    qs_ref[0] = jnp.where(first_head(p), qf, 0.0).astype(BF16)
    qs_ref[1] = jnp.where(first_head(p), 0.0, qf).astype(BF16)

    def q_rows(i):
        return slice(CHUNK + i * tq, CHUNK + (i + 1) * tq)

    def k_rows(j):
        return slice(CHUNK + j * tk, CHUNK + (j + 1) * tk)

    meta_rows = slice(0, CHUNK)
    heads = range(2)

    k_meta = k_ref[0, PAD_ROWS:CHUNK, :]
    v_meta = v_ref[0, PAD_ROWS:CHUNK, :]
    jr, jc = iotas(N_META, N_META)
    newer_meta = (jc > jr).astype(BF16)
    pv_meta = []
    for hd in heads:
        zt = _dot_nt(k_meta, qs_ref[hd, CHUNK:, :])
        spt = jnp.maximum(zt, 0.0) + jnp.log(1.0 + jnp.exp2(jnp.abs(zt) * (-LOG2E)))
        wt = jnp.exp(zt - spt - _dot(newer_meta, spt.astype(BF16))).astype(BF16)
        pv_meta.append(lax.dot_general(wt, v_meta, (((0,), (0,)), ((), ())), preferred_element_type=F32))

    tiles = [(meta_rows, meta_rows, meta_mask, upper_c, "only")]
    for i in range(nq):
        own = i // 2
        role = "first+last" if own == 0 else "first"
        if i % 2 == 0:
            tiles.append((q_rows(i), q_rows(i), strict, upper_c, role))
        else:
            tiles.append((q_rows(i), k_rows(own), strict_second, upper_k, role))
        tiles += [(q_rows(i), k_rows(j), None, upper_k, "last" if j == 0 else "middle")
                  for j in range(own - 1, -1, -1)]
    scored, weighted = {}, {}
    acc, car = [None, None], [None, None]

    def scores(t):
        qrows, krows, mask, _, _ = tiles[t]
        k16 = k_ref[0, krows, :]
        zs = [_dot_nt(qs_ref[hd, qrows, :], k16) for hd in heads]
        for hd in heads:
            z = zs[hd]
            sp = jnp.maximum(z, 0.0) + jnp.log(1.0 + jnp.exp2(jnp.abs(z) * (-LOG2E)))
            if mask is not None:
                sp = jnp.where(mask, sp, 0.0)
            scored[t, hd] = (sp.astype(BF16), z - sp, sp[:, 0:1])

    def weights(t):
        _, _, mask, upper, _ = tiles[t]
        newer = [_dot(scored[t, hd][0], upper) for hd in heads]
        for hd in heads:
            _, logsig, sp0 = scored.pop((t, hd))
            w = jnp.exp(logsig - newer[hd])
            if mask is not None:
                w = jnp.where(mask, w, 0.0)
            weighted[t, hd] = (w.astype(BF16), jnp.broadcast_to(newer[hd][:, 0:1] + sp0, (w.shape[0], LANE)))

    def products(t):
        qrows, krows, _, _, role = tiles[t]
        v16 = v_ref[0, krows, :]
        pvs = [_dot(weighted[t, hd][0], v16) for hd in heads]
        for hd in heads:
            _, rowsum = weighted.pop((t, hd))
            if role in ("only", "first", "first+last"):
                acc[hd], car[hd] = pvs[hd], rowsum
            else:
                acc[hd] = acc[hd] + pvs[hd] * jnp.exp(-car[hd])
                car[hd] = car[hd] + rowsum
            if role in ("last", "first+last"):
                meta = pv_meta[hd][qrows.start - CHUNK:qrows.stop - CHUNK, :]
                acc[hd] = acc[hd] + meta * jnp.exp(-car[hd])
        if role in ("only", "last", "first+last"):
            o_ref[0, qrows, :] = jnp.where(first_head(acc[0].shape[0]), acc[0], acc[1]).astype(o_ref.dtype)

    for t in range(len(tiles) + 2):
        if t < len(tiles):
            scores(t)
        if 1 <= t <= len(tiles):
            weights(t - 1)
        if t >= 2:
            products(t - 2)


def _sb_attention(u):
    bsz, p, _ = u.shape
    pairs = SB_WIDTH // LANE
    nq = (p - CHUNK) // CHUNK
    assert (p - CHUNK) % SB_TILE == 0

    def ublock(base):
        return pl.BlockSpec((1, p, LANE), lambda b, hp: (b, 0, base + hp))

    return pl.pallas_call(
        functools.partial(_sb_kernel, nq=nq),
        grid=(bsz, pairs),
        in_specs=[ublock(0), ublock(pairs), ublock(2 * pairs)],
        out_specs=pl.BlockSpec((1, p, LANE), lambda b, hp: (b, 0, hp)),
        out_shape=jax.ShapeDtypeStruct((bsz, p, SB_WIDTH), BF16),
        scratch_shapes=[pltpu.VMEM((2, p, LANE), BF16)],
        compiler_params=_params(("parallel", "parallel"), 48),
        name="stickbreak_attention",
    )(u, u, u)


def _gelu_tanh(x):
    return 0.5 * x * (1.0 + jnp.tanh(math.sqrt(2.0 / math.pi) * (x + 0.044715 * (x * x * x))))


def _lru_kernel(gate_ref, xr_ref, cw_ref, cb_ref, wax_ref, ba_ref, bx_ref, lam_ref, o_ref, hist):
    L = CHUNK
    w = LRU_WIDTH
    tiles = o_ref.shape[1] // L
    hist[0:L, :] = jnp.zeros((L, w), BF16)
    soft_lam = _softplus(-lam_ref[...])
    sub = lax.broadcasted_iota(jnp.int32, (1, HALO, 1), 1)
    groups = L // HALO
    carry = jnp.zeros((1, w), F32)
    for t in range(tiles):
        rows = slice(t * L, (t + 1) * L)
        x = _causal_conv(hist, xr_ref[0, rows, :], cw_ref, cb_ref, LRU_CONV)
        x16 = x.astype(BF16)
        gates = [_dot(x16[:, n * LRU_BLOCK:(n + 1) * LRU_BLOCK], wax_ref[n]) for n in range(LRU_BLOCKS)]
        r = _sigmoid(jnp.concatenate([g[:, :LRU_BLOCK] for g in gates], axis=-1) + ba_ref[...])
        i = _sigmoid(jnp.concatenate([g[:, LRU_BLOCK:] for g in gates], axis=-1) + bx_ref[...])
        log_a = -LRU_C * r * soft_lam
        a = jnp.exp(log_a)
        gap = 1.0 - jnp.exp(2.0 * log_a)
        b = jnp.where(gap > 0.0, gap * lax.rsqrt(gap), 0.0) * (i * x)
        if t == 0:
            b = jnp.where(lax.broadcasted_iota(jnp.int32, (L, 1), 0) >= PAD_ROWS, b, 0.0)

        a = a.reshape(groups, HALO, w)
        b = b.reshape(groups, HALO, w)
        k = 1
        while k < HALO:
            keep = sub >= k
            a_sh = jnp.where(keep, pltpu.roll(a, k, 1), 1.0)
            b_sh = jnp.where(keep, pltpu.roll(b, k, 1), 0.0)
            b = a * b_sh + b
            a = a * a_sh
            k *= 2
        hs = []
        for g in range(groups):
            hg = a[g] * carry + b[g]
            hs.append(hg)
            carry = hg[HALO - 1:HALO, :]
        hs = jnp.concatenate(hs, axis=0)
        o_ref[0, rows, :] = (hs * _gelu_tanh(gate_ref[0, rows, :].astype(F32))).astype(o_ref.dtype)


def _rg_lru(u, conv_w, conv_b, w_ax, b_a, b_x, lam):
    bsz, p, _ = u.shape
    w = LRU_WIDTH

    def full(shape):
        return pl.BlockSpec(shape, lambda b: (0,) * len(shape))

    return pl.pallas_call(
        _lru_kernel,
        grid=(bsz,),
        in_specs=[
            pl.BlockSpec((1, p, w), lambda b: (b, 0, 3)),
            pl.BlockSpec((1, p, w), lambda b: (b, 0, 4)),
            full((LRU_CONV, w)), full((1, w)), full((LRU_BLOCKS, LRU_BLOCK, 2 * LRU_BLOCK)),
            full((1, w)), full((1, w)), full((1, w)),
        ],
        out_specs=pl.BlockSpec((1, p, w), lambda b: (b, 0, 0)),
        out_shape=jax.ShapeDtypeStruct((bsz, p, w), BF16),
        scratch_shapes=[pltpu.VMEM((2 * CHUNK, w), BF16)],
        compiler_params=_params(("parallel",), 48),
        name="rg_lru",
    )(u, u, conv_w, conv_b.reshape(1, w), w_ax, b_a.reshape(1, w), b_x.reshape(1, w), lam.reshape(1, w))


def _final_norm_kernel(h_ref, g_ref, o_ref):
    x = h_ref[0, CHUNK:, :]
    ms = jnp.mean(x * x, axis=-1, keepdims=True)
    o_ref[0] = (x * lax.rsqrt(ms + EPS)) * g_ref[...]


def _final_norm(h, gain):
    bsz, p, d = h.shape
    return pl.pallas_call(
        _final_norm_kernel,
        grid=(bsz,),
        in_specs=[pl.BlockSpec((1, p, d), lambda b: (b, 0, 0)), pl.BlockSpec((1, d), lambda b: (0, 0))],
        out_specs=pl.BlockSpec((1, p - CHUNK, d), lambda b: (b, 0, 0)),
        out_shape=jax.ShapeDtypeStruct((bsz, p - CHUNK, d), F32),
        compiler_params=_params(("parallel",), 48),
        name="final_norm",
    )(h, gain.reshape(1, d))


def kernel(x, meta_tokens, l0_mix_norm, l0_w_in, l0_ssd_conv_w, l0_ssd_conv_b, l0_ssd_dt_bias, l0_ssd_a_log,
           l0_ssd_d, l0_ssd_norm, l0_ret_norm, l0_w_out, l0_ffn_norm, l0_ffn_w_in, l0_ffn_conv_w, l0_ffn_conv_b,
           l0_ffn_w_out, l1_mix_norm, l1_w_in, l1_lru_conv_w, l1_lru_conv_b, l1_lru_wa, l1_lru_ba, l1_lru_wx,
           l1_lru_bx, l1_lru_lambda, l1_w_out, l1_ffn_norm, l1_ffn_w_in, l1_ffn_conv_w, l1_ffn_conv_b,
           l1_ffn_w_out, final_norm):
    bsz, seq, d = x.shape
    p = seq + CHUNK
    assert d == D_MODEL and seq % SB_TILE == 0

    lead = jnp.concatenate([jnp.zeros((PAD_ROWS, d), x.dtype), meta_tokens.astype(x.dtype)], 0)
    h = jnp.concatenate([jnp.broadcast_to(lead[None], (bsz, CHUNK, d)), x], 1)

    dt0 = SSD_INNER + SSD_INNER + 2 * SSD_GROUPS * SSD_STATE
    w_main = jnp.concatenate([l0_w_in[:, :dt0], l0_w_in[:, dt0 + SSD_HEADS:]], 1).astype(BF16)
    w_dt = jnp.pad(l0_w_in[:, dt0:dt0 + SSD_HEADS], ((0, 0), (0, LANE - SSD_HEADS))).astype(BF16)
    u, dt_raw = _norm_matmul(h.reshape(bsz * p, d), l0_mix_norm, w_main, w_dt)
    u = u.reshape(bsz, p, -1)
    y_ssd = _ssd(u, dt_raw.reshape(bsz, p, LANE), l0_ssd_conv_w, l0_ssd_conv_b, l0_ssd_dt_bias, l0_ssd_a_log,
                 l0_ssd_d, l0_ssd_norm)
    y_ret = _retention(u, l0_ret_norm)
    w_out = l0_w_out.astype(BF16)
    h = _mix_ffn(y_ssd, y_ret, w_out[:SSD_INNER], w_out[SSD_INNER:], h, l0_ffn_norm, l0_ffn_w_in.astype(BF16),
                 l0_ffn_conv_w, l0_ffn_conv_b, l0_ffn_w_out.astype(BF16))

    u = _norm_matmul(h.reshape(bsz * p, d), l1_mix_norm, l1_w_in.astype(BF16)).reshape(bsz, p, -1)
    y_sb = _sb_attention(u)
    w_ax = jnp.concatenate([l1_lru_wa, l1_lru_wx], -1).astype(BF16)
    y_lru = _rg_lru(u, l1_lru_conv_w, l1_lru_conv_b, w_ax, l1_lru_ba, l1_lru_bx, l1_lru_lambda)
    w_out = l1_w_out.astype(BF16)
    h = _mix_ffn(y_sb, y_lru, w_out[:SB_WIDTH], w_out[SB_WIDTH:], h, l1_ffn_norm, l1_ffn_w_in.astype(BF16),
                 l1_ffn_conv_w, l1_ffn_conv_b, l1_ffn_w_out.astype(BF16))

    return _final_norm(h, final_norm)
```
